```python
import jax, jax.numpy as jnp
from jax import lax
import numpy as np

D_MODEL = 2048
BATCH = 8
SEQ = 4096
DEPTH = 2

N_A_LAYERS = DEPTH // 2
N_B_LAYERS = DEPTH - N_A_LAYERS
A_HEADS = 16
A_KEY_DIM = D_MODEL // A_HEADS
A_VAL_DIM = D_MODEL // A_HEADS
A_CHUNK = 16
B_HEADS = 16
B_KV_HEADS = 4
B_HEAD_DIM = D_MODEL // B_HEADS
B_GROUP = B_HEADS // B_KV_HEADS
Q_BLOCK = 128
D_FF = 4 * D_MODEL
NORM_EPS = 1e-6

kernel_name = 'yoco_hgrn2_stickbreaking_adaln_block'


def rms_norm(x, gain):
    xf = x.astype(jnp.float32)
    inv = lax.rsqrt(jnp.mean(xf * xf, axis=-1, keepdims=True) + NORM_EPS)
    return (xf * inv).astype(x.dtype) * gain


def modulate(h, shift, scale):
    return h * (1 + scale[:, None, :]) + shift[:, None, :]


def squared_relu_mlp(h, w1, w2):
    a = jax.nn.relu(h @ w1)
    return (a * a) @ w2


def hgrn2_mixer(h, w_in, lb, out_gain, w_out):
    b, s, _ = h.shape
    proj = h @ w_in
    q, f_logit, i_in, g = jnp.split(proj, 4, axis=-1)
    q = jax.nn.silu(q.astype(jnp.float32))
    f_logit = f_logit.astype(jnp.float32)
    f = lb + (1 - lb) * jax.nn.sigmoid(f_logit)
    log_f = jnp.log(f)
    k = (1 - lb) * jax.nn.sigmoid(-f_logit)
    v = i_in.astype(jnp.float32)
    nc = s // A_CHUNK
    shp = (b, nc, A_CHUNK, A_HEADS, A_KEY_DIM)
    q = q.reshape(shp)
    k = k.reshape(shp)
    log_f = log_f.reshape(shp)
    v = v.reshape(b, nc, A_CHUNK, A_HEADS, A_VAL_DIM)
    cum = jnp.cumsum(log_f, axis=2)
    q_dec = q * jnp.exp(cum)
    k_intra = k * jnp.exp(-cum)
    k_state = k * jnp.exp(cum[:, :, -1:] - cum)
    chunk_decay = jnp.exp(cum[:, :, -1])
    causal = jnp.tril(jnp.ones((A_CHUNK, A_CHUNK), dtype=bool))
    scores = jnp.einsum('bnthk,bnshk->bnhts', q_dec, k_intra)
    scores = jnp.where(causal, scores, 0.0)
    o_intra = jnp.einsum('bnhts,bnshv->bnthv', scores, v)

    def step(state, inp):
        q_c, k_c, v_c, dec_c = inp
        o_c = jnp.einsum('bthk,bhkv->bthv', q_c, state)
        state = dec_c[..., None] * state + jnp.einsum('bthk,bthv->bhkv', k_c, v_c)
        return state, o_c

    xs = (jnp.moveaxis(q_dec, 1, 0), jnp.moveaxis(k_state, 1, 0),
          jnp.moveaxis(v, 1, 0), jnp.moveaxis(chunk_decay, 1, 0))
    s0 = jnp.zeros((b, A_HEADS, A_KEY_DIM, A_VAL_DIM), jnp.float32)
    _, o_inter = lax.scan(step, s0, xs)
    o = (o_intra + jnp.moveaxis(o_inter, 0, 1)).reshape(b, s, A_HEADS, A_VAL_DIM)
    o = o * lax.rsqrt(jnp.mean(o * o, axis=-1, keepdims=True) + NORM_EPS)
    o = o.reshape(b, s, A_HEADS * A_VAL_DIM) * out_gain
    o = o * jax.nn.silu(g.astype(jnp.float32))
    return o.astype(h.dtype) @ w_out


def stick_breaking_mixer(h, w_q, k, v, w_out):
    b, s, _ = h.shape
    q = (h @ w_q).reshape(b, s, B_KV_HEADS, B_GROUP, B_HEAD_DIM)
    scale = B_HEAD_DIM ** -0.5
    outs = []
    for blk in range(s // Q_BLOCK):
        t0 = blk * Q_BLOCK
        t1 = t0 + Q_BLOCK
        q_blk = q[:, t0:t1]
        k_pre = k[:, :t1]
        v_pre = v[:, :t1]
        z = jnp.einsum('btkgd,bskd->bkgts', q_blk, k_pre).astype(jnp.float32) * scale
        t_pos = t0 + jnp.arange(Q_BLOCK)[:, None]
        s_pos = jnp.arange(t1)[None, :]
        mask = s_pos < t_pos
        log_beta = jax.nn.log_sigmoid(z)
        log_rest = jnp.where(mask, log_beta - z, 0.0)
        between = lax.cumsum(log_rest, axis=4, reverse=True) - log_rest
        weights = jnp.where(mask, jnp.exp(log_beta + between), 0.0)
        outs.append(jnp.einsum('bkgts,bskd->btkgd', weights.astype(v.dtype), v_pre))
    o = jnp.concatenate(outs, axis=1).reshape(b, s, B_HEADS * B_HEAD_DIM)
    return o @ w_out


def setup_inputs(seed: int = 0) -> dict:
    key = jax.random.key(seed)
    ks = jax.random.split(key, 20)
    d = D_MODEL
    f32 = jnp.float32

    def nrm(k, shape, fan_in, gain=1.0):
        return jax.random.normal(k, shape, f32) * (gain * fan_in ** -0.5)

    def small(k, shape):
        return 0.02 * jax.random.normal(k, shape, f32)

    return {
        'x': jax.random.normal(ks[0], (BATCH, SEQ, d), f32),
        'c': jax.random.normal(ks[1], (BATCH, d), f32),
        'ada_w': nrm(ks[2], (DEPTH, d, 6 * d), d, 0.5),
        'ada_b': small(ks[3], (DEPTH, 6 * d)),
        'norm_mix': 1.0 + small(ks[4], (DEPTH, d)),
        'norm_mlp': 1.0 + small(ks[5], (DEPTH, d)),
        'a_w_in': nrm(ks[6], (N_A_LAYERS, d, 4 * d), d),
        'a_lb_logits': 0.1 * jax.random.normal(ks[7], (N_A_LAYERS + 1, A_HEADS * A_KEY_DIM), f32),
        'a_out_gain': 1.0 + small(ks[8], (N_A_LAYERS, A_HEADS * A_VAL_DIM)),
        'a_w_out': nrm(ks[9], (N_A_LAYERS, A_HEADS * A_VAL_DIM, d), A_HEADS * A_VAL_DIM),
        'kv_ada_w': nrm(ks[10], (d, 2 * d), d, 0.5),
        'kv_ada_b': small(ks[11], (2 * d,)),
        'kv_norm': 1.0 + small(ks[12], (d,)),
        'w_kv': nrm(ks[13], (d, 2 * B_KV_HEADS * B_HEAD_DIM), d),
        'b_w_q': nrm(ks[14], (N_B_LAYERS, d, B_HEADS * B_HEAD_DIM), d),
        'b_w_out': nrm(ks[15], (N_B_LAYERS, B_HEADS * B_HEAD_DIM, d), B_HEADS * B_HEAD_DIM),
        'mlp_w1': nrm(ks[16], (DEPTH, d, D_FF), d),
        'mlp_w2': nrm(ks[17], (DEPTH, D_FF, d), D_FF),
        'final_norm': 1.0 + small(ks[18], (d,)),
    }


def reference(x, c, ada_w, ada_b, norm_mix, norm_mlp, a_w_in, a_lb_logits, a_out_gain, a_w_out,
              kv_ada_w, kv_ada_b, kv_norm, w_kv, b_w_q, b_w_out, mlp_w1, mlp_w2, final_norm):
    b, s, _ = x.shape
    c_act = jax.nn.silu(c)
    lb_all = jnp.cumsum(jax.nn.softmax(a_lb_logits.astype(jnp.float32), axis=0), axis=0)
    k_shared = None
    v_shared = None
    for layer in range(DEPTH):
        mod = c_act @ ada_w[layer] + ada_b[layer]
        sh1, sc1, g1, sh2, sc2, g2 = jnp.split(mod, 6, axis=-1)
        h = modulate(rms_norm(x, norm_mix[layer]), sh1, sc1)
        if layer < N_A_LAYERS:
            y = hgrn2_mixer(h, a_w_in[layer], lb_all[layer], a_out_gain[layer], a_w_out[layer])
        else:
            j = layer - N_A_LAYERS
            y = stick_breaking_mixer(h, b_w_q[j], k_shared, v_shared, b_w_out[j])
        x = x + g1[:, None, :] * y
        h = modulate(rms_norm(x, norm_mlp[layer]), sh2, sc2)
        x = x + g2[:, None, :] * squared_relu_mlp(h, mlp_w1[layer], mlp_w2[layer])
        if layer == N_A_LAYERS - 1:
            kv_sh, kv_sc = jnp.split(c_act @ kv_ada_w + kv_ada_b, 2, axis=-1)
            hk = modulate(rms_norm(x, kv_norm), kv_sh, kv_sc)
            kv = (hk @ w_kv).reshape(b, s, 2, B_KV_HEADS, B_HEAD_DIM)
            k_shared = kv[:, :, 0]
            v_shared = kv[:, :, 1]
    return rms_norm(x, final_norm)
```

```python
import functools

import jax
import jax.numpy as jnp
from jax import lax
from jax.experimental import pallas as pl
from jax.experimental.pallas import tpu as pltpu

HEAD_DIM = 128
NORM_EPS = 1e-6
HGRN_CHUNK = 32
HGRN_HEADS_PER_STEP = 4
ATTN_BLOCK = 128
VMEM_LIMIT_BYTES = 56 * 1024 * 1024

F32 = jnp.float32
BF16 = jnp.bfloat16


def _tile(n, pref):
    t = min(n, pref)
    while n % t:
        t -= 1
    return t


def _params(*sem):
    return pltpu.CompilerParams(dimension_semantics=sem, vmem_limit_bytes=VMEM_LIMIT_BYTES)


def _split_bf16(a):
    hi = a.astype(BF16)
    lo = (a - hi.astype(F32)).astype(BF16)
    return hi, lo


def _rms_mod(x, gain, shift, scale):
    inv = lax.rsqrt(jnp.mean(x * x, axis=-1, keepdims=True) + NORM_EPS)
    return (x * inv * gain) * (1.0 + scale) + shift


def _cond_kernel(c_ref, w_ref, b_ref, o_ref):
    c = c_ref[...]
    c_act = c * (1.0 / (1.0 + jnp.exp(-c)))
    c_hi, c_lo = _split_bf16(c_act)
    w_hi, w_lo = _split_bf16(w_ref[...])
    lhs = jnp.concatenate([c_hi, c_lo], axis=0)
    acc = jnp.dot(lhs, w_hi, preferred_element_type=F32)
    nb = c.shape[0]
    out = acc[:nb] + acc[nb:] + jnp.dot(c_hi, w_lo, preferred_element_type=F32)
    o_ref[...] = out + b_ref[...]


def _conditioning(c, w, b):
    nl, d, n = w.shape
    nb = c.shape[0]
    tn = _tile(n, 1024)
    return pl.pallas_call(
        _cond_kernel,
        grid=(nl, n // tn),
        in_specs=[
            pl.BlockSpec((nb, d), lambda l, j: (0, 0)),
            pl.BlockSpec((None, d, tn), lambda l, j: (l, 0, j)),
            pl.BlockSpec((None, 1, tn), lambda l, j: (l, 0, j)),
        ],
        out_specs=pl.BlockSpec((None, nb, tn), lambda l, j: (l, 0, j)),
        out_shape=jax.ShapeDtypeStruct((nl, nb, n), F32),
        compiler_params=_params("parallel", "parallel"),
        name="conditioning",
    )(c, w, b.reshape(nl, 1, n))


def _norm_proj_kernel(x_ref, gain_ref, shift_ref, scale_ref, w_ref, o_ref, h_ref, *, out_scale):
    @pl.when(pl.program_id(1) == 0)
    def _():
        h = _rms_mod(x_ref[...], gain_ref[...], shift_ref[0], scale_ref[0])
        h_ref[...] = h.astype(BF16)

    acc = jnp.dot(h_ref[...], w_ref[...], preferred_element_type=F32)
    if out_scale != 1.0:
        acc = acc * out_scale
    o_ref[...] = acc.astype(o_ref.dtype)


def _norm_proj(x2, gain, shift, scale, w, seq, out_dtype, out_scale=1.0):
    t, d = x2.shape
    n = w.shape[1]
    tm = _tile(seq, 512)
    tn = _tile(n, 1024)
    per_b = seq // tm
    return pl.pallas_call(
        functools.partial(_norm_proj_kernel, out_scale=out_scale),
        grid=(t // tm, n // tn),
        in_specs=[
            pl.BlockSpec((tm, d), lambda i, j: (i, 0)),
            pl.BlockSpec((1, d), lambda i, j: (0, 0)),
            pl.BlockSpec((1, 1, d), lambda i, j: (i // per_b, 0, 0)),
            pl.BlockSpec((1, 1, d), lambda i, j: (i // per_b, 0, 0)),
            pl.BlockSpec((d, tn), lambda i, j: (0, j)),
        ],
        out_specs=pl.BlockSpec((tm, tn), lambda i, j: (i, j)),
        out_shape=jax.ShapeDtypeStruct((t, n), out_dtype),
        scratch_shapes=[pltpu.VMEM((tm, d), BF16)],
        compiler_params=_params("parallel", "arbitrary"),
        name="norm_proj",
    )(x2, gain.reshape(1, d), shift, scale, w)


def _proj_residual_kernel(a_ref, w_ref, x_ref, gate_ref, o_ref):
    y = jnp.dot(a_ref[...], w_ref[...], preferred_element_type=F32)
    o_ref[...] = x_ref[...] + gate_ref[0] * y


def _proj_residual(a, w, x2, gate, seq):
    t, k = a.shape
    d = w.shape[1]
    tm = _tile(seq, 512)
    per_b = seq // tm
    return pl.pallas_call(
        _proj_residual_kernel,
        grid=(t // tm,),
        in_specs=[
            pl.BlockSpec((tm, k), lambda i: (i, 0)),
            pl.BlockSpec((k, d), lambda i: (0, 0)),
            pl.BlockSpec((tm, d), lambda i: (i, 0)),
            pl.BlockSpec((1, 1, d), lambda i: (i // per_b, 0, 0)),
        ],
        out_specs=pl.BlockSpec((tm, d), lambda i: (i, 0)),
        out_shape=jax.ShapeDtypeStruct((t, d), F32),
        compiler_params=_params("parallel"),
        name="proj_residual",
    )(a, w, x2, gate)


def _mlp_kernel(x_ref, gain_ref, shift_ref, scale_ref, gate_ref, w1_ref, w2_ref, fgain_ref,
                o_ref, h_ref, acc_ref, *, final_norm):
    j = pl.program_id(1)

    @pl.when(j == 0)
    def _():
        h = _rms_mod(x_ref[...], gain_ref[...], shift_ref[0], scale_ref[0])
        h_ref[...] = h.astype(BF16)
        acc_ref[...] = jnp.zeros_like(acc_ref)

    a = jnp.dot(h_ref[...], w1_ref[...], preferred_element_type=F32)
    a = jnp.maximum(a, 0.0)
    a = (a * a).astype(BF16)
    acc_ref[...] += jnp.dot(a, w2_ref[...], preferred_element_type=F32)

    @pl.when(j == pl.num_programs(1) - 1)
    def _():
        y = x_ref[...] + gate_ref[0] * acc_ref[...]
        if final_norm:
            inv = lax.rsqrt(jnp.mean(y * y, axis=-1, keepdims=True) + NORM_EPS)
            y = y * inv * fgain_ref[...]
        o_ref[...] = y


def _mlp(x2, gain, shift, scale, gate, w1, w2, fgain, seq, final_norm):
    t, d = x2.shape
    f = w1.shape[1]
    tm = _tile(seq, 512)
    tf = _tile(f, 1024)
    per_b = seq // tm
    return pl.pallas_call(
        functools.partial(_mlp_kernel, final_norm=final_norm),
        grid=(t // tm, f // tf),
        in_specs=[
            pl.BlockSpec((tm, d), lambda i, j: (i, 0)),
            pl.BlockSpec((1, d), lambda i, j: (0, 0)),
            pl.BlockSpec((1, 1, d), lambda i, j: (i // per_b, 0, 0)),
            pl.BlockSpec((1, 1, d), lambda i, j: (i // per_b, 0, 0)),
            pl.BlockSpec((1, 1, d), lambda i, j: (i // per_b, 0, 0)),
            pl.BlockSpec((d, tf), lambda i, j: (0, j)),
            pl.BlockSpec((tf, d), lambda i, j: (j, 0)),
            pl.BlockSpec((1, d), lambda i, j: (0, 0)),
        ],
        out_specs=pl.BlockSpec((tm, d), lambda i, j: (i, 0)),
        out_shape=jax.ShapeDtypeStruct((t, d), F32),
        scratch_shapes=[pltpu.VMEM((tm, d), BF16), pltpu.VMEM((tm, d), F32)],
        compiler_params=_params("parallel", "arbitrary"),
        name="mlp",
    )(x2, gain.reshape(1, d), shift, scale, gate, w1, w2, fgain.reshape(1, d))


def _hgrn_kernel(q_ref, f_ref, i_ref, g_ref, lbl_ref, og_ref, o_ref, st_ref, *,
                 heads, chunk, n_chunks, layer):
    hd = HEAD_DIM
    half = chunk // 2

    @pl.when(pl.program_id(2) == 0)
    def _():
        st_ref[...] = jnp.zeros_like(st_ref)

    logits = lbl_ref[...]
    ex = jnp.exp(logits - jnp.max(logits, axis=0, keepdims=True))
    lb = jnp.sum(ex[:layer + 1], axis=0, keepdims=True) / jnp.sum(ex, axis=0, keepdims=True)
    one_m_lb = 1.0 - lb
    og = og_ref[...]
    row = lax.broadcasted_iota(jnp.int32, (chunk, heads * hd), 0)
    ti = lax.broadcasted_iota(jnp.int32, (chunk, chunk), 0)
    si = lax.broadcasted_iota(jnp.int32, (chunk, chunk), 1)
    causal = si <= ti

    def body(c, carry):
        r0 = pl.multiple_of(c * chunk, chunk)
        rows = pl.ds(r0, chunk)
        z = f_ref[rows, :]
        e = jnp.exp(-jnp.abs(z))
        r = 1.0 / (1.0 + e)
        er = e * r
        pos = z >= 0.0
        sig = jnp.where(pos, r, er)
        nsig = jnp.where(pos, er, r)
        log_f = jnp.log(lb + one_m_lb * sig)
        k = one_m_lb * nsig
        cum = log_f
        d = 1
        while d < chunk:
            cum = cum + jnp.where(row >= d, pltpu.roll(cum, d, 0), 0.0)
            d *= 2
        mid = cum[half - 1:half, :]
        tot = cum[chunk - 1:chunk, :]
        qv = q_ref[rows, :]
        qs = qv * (1.0 / (1.0 + jnp.exp(-qv)))
        q_in = (qs * jnp.exp(cum - mid)).astype(BF16)
        k_in = (k * jnp.exp(mid - cum)).astype(BF16)
        q_dec = (qs * jnp.exp(cum)).astype(BF16)
        k_st = (k * jnp.exp(tot - cum)).astype(BF16)
        dec = jnp.exp(tot)
        v = i_ref[rows, :].astype(BF16)
        gv = g_ref[rows, :]
        gs = gv * (1.0 / (1.0 + jnp.exp(-gv)))
        for h in range(heads):
            cs = slice(h * hd, (h + 1) * hd)
            sc = lax.dot_general(q_in[:, cs], k_in[:, cs], (((1,), (1,)), ((), ())),
                                 preferred_element_type=F32)
            sc = jnp.where(causal, sc, 0.0).astype(BF16)
            st = st_ref[h]
            o = jnp.dot(sc, v[:, cs], preferred_element_type=F32)
            o = o + lax.dot_general(q_dec[:, cs], st.astype(BF16), (((1,), (1,)), ((), ())),
                                    preferred_element_type=F32)
            upd = lax.dot_general(v[:, cs], k_st[:, cs], (((0,), (0,)), ((), ())),
                                  preferred_element_type=F32)
            st_ref[h] = dec[:, cs] * st + upd
            inv = lax.rsqrt(jnp.mean(o * o, axis=-1, keepdims=True) + NORM_EPS)
            o_ref[rows, cs] = (o * inv * og[:, cs] * gs[:, cs]).astype(o_ref.dtype)
        return carry

    lax.fori_loop(0, n_chunks, body, 0)


def _hgrn(proj, lb_logits, layer, out_gain, batch, seq):
    t, d4 = proj.shape
    d = d4 // 4
    slots = lb_logits.shape[0]
    hb = HGRN_HEADS_PER_STEP
    w = hb * HEAD_DIM
    groups = d // w
    tc = _tile(seq, 512)
    nt = seq // tc

    def part(p):
        return pl.BlockSpec((tc, w), lambda b, hg, s: (b * nt + s, p * groups + hg))

    vec = pl.BlockSpec((1, w), lambda b, hg, s: (0, hg))
    slot_vec = pl.BlockSpec((slots, w), lambda b, hg, s: (0, hg))
    return pl.pallas_call(
        functools.partial(_hgrn_kernel, heads=hb, chunk=HGRN_CHUNK, n_chunks=tc // HGRN_CHUNK,
                          layer=layer),
        grid=(batch, groups, nt),
        in_specs=[part(0), part(1), part(2), part(3), slot_vec, vec],
        out_specs=pl.BlockSpec((tc, w), lambda b, hg, s: (b * nt + s, hg)),
        out_shape=jax.ShapeDtypeStruct((t, d), BF16),
        scratch_shapes=[pltpu.VMEM((hb, HEAD_DIM, HEAD_DIM), F32)],
        compiler_params=_params("parallel", "parallel", "arbitrary"),
        name="hgrn",
    )(proj, proj, proj, proj, lb_logits.astype(F32), out_gain.reshape(1, d))


def _attn_kernel(q_ref, k_ref, v_ref, u_ref, o_ref, *, group, blk):
    hd = HEAD_DIM
    qi = pl.program_id(2)
    rows = group * blk
    q4 = jnp.concatenate([q_ref[:, g * hd:(g + 1) * hd] for g in range(group)], axis=0)
    uu = u_ref[...]
    t_in = lax.broadcasted_iota(jnp.int32, (rows, blk), 0) % blk
    s_in = lax.broadcasted_iota(jnp.int32, (rows, blk), 1)
    strict = s_in < t_in

    def step(kb, run, acc, mask):
        ks = pl.ds(pl.multiple_of(kb * blk, blk), blk)
        z = lax.dot_general(q4, k_ref[ks, :], (((1,), (1,)), ((), ())), preferred_element_type=F32)
        ls = -(jnp.maximum(z, 0.0) + jnp.log(1.0 + jnp.exp(-jnp.abs(z))))
        if mask is not None:
            ls = jnp.where(mask, ls, 0.0)
        hi, lo = _split_bf16(ls)
        cs = jnp.dot(jnp.concatenate([hi, lo], axis=1), uu, preferred_element_type=F32)
        w = jnp.exp(z + cs[:, :blk] + run)
        if mask is not None:
            w = jnp.where(mask, w, 0.0)
        acc = acc + jnp.dot(w.astype(BF16), v_ref[ks, :], preferred_element_type=F32)
        return run + cs[:, blk:], acc

    run0 = jnp.zeros((rows, blk), F32)
    acc0 = jnp.zeros((rows, hd), F32)
    run, acc = step(qi, run0, acc0, strict)

    def body(n, carry):
        return step(qi - 1 - n, carry[0], carry[1], None)

    run, acc = lax.fori_loop(0, qi, body, (run, acc))
    for g in range(group):
        o_ref[:, g * hd:(g + 1) * hd] = acc[g * blk:(g + 1) * blk, :].astype(o_ref.dtype)


def _attention(q, kv, batch, seq):
    t, d = q.shape
    hd = HEAD_DIM
    kvh = kv.shape[1] // (2 * hd)
    group = d // hd // kvh
    blk = _tile(seq, ATTN_BLOCK)
    nq = seq // blk
    ji = lax.broadcasted_iota(jnp.int32, (blk, blk), 0)
    si = lax.broadcasted_iota(jnp.int32, (blk, blk), 1)
    u = jnp.concatenate([(ji >= si).astype(BF16), jnp.ones((blk, blk), BF16)], axis=1)
    uu = jnp.concatenate([u, u], axis=0)
    return pl.pallas_call(
        functools.partial(_attn_kernel, group=group, blk=blk),
        grid=(batch, kvh, nq),
        in_specs=[
            pl.BlockSpec((blk, group * hd), lambda b, h, i: (b * nq + i, h)),
            pl.BlockSpec((seq, hd), lambda b, h, i: (b, h)),
            pl.BlockSpec((seq, hd), lambda b, h, i: (b, kvh + h)),
            pl.BlockSpec((2 * blk, 2 * blk), lambda b, h, i: (0, 0)),
        ],
        out_specs=pl.BlockSpec((blk, group * hd), lambda b, h, i: (b * nq + i, h)),
        out_shape=jax.ShapeDtypeStruct((t, d), BF16),
        compiler_params=_params("parallel", "parallel", "arbitrary"),
        name="stick_breaking",
    )(q, kv, kv, uu)


def kernel(x, c, ada_w, ada_b, norm_mix, norm_mlp, a_w_in, a_lb_logits, a_out_gain, a_w_out,
           kv_ada_w, kv_ada_b, kv_norm, w_kv, b_w_q, b_w_out, mlp_w1, mlp_w2, final_norm):
    b, s, d = x.shape
    depth = ada_w.shape[0]
    n_a = a_w_in.shape[0]
    x2 = x.reshape(b * s, d)

    mod = _conditioning(c, ada_w, ada_b).reshape(depth, b, 6, 1, d)
    kv_mod = _conditioning(c, kv_ada_w[None], kv_ada_b[None]).reshape(b, 2, 1, d)

    kv = None
    for layer in range(depth):
        sh1, sc1, g1, sh2, sc2, g2 = (mod[layer, :, p] for p in range(6))
        if layer < n_a:
            proj = _norm_proj(x2, norm_mix[layer], sh1, sc1, a_w_in[layer].astype(BF16), s, F32)
            o = _hgrn(proj, a_lb_logits, layer, a_out_gain[layer], b, s)
            w_out = a_w_out[layer]
        else:
            j = layer - n_a
            q = _norm_proj(x2, norm_mix[layer], sh1, sc1, b_w_q[j].astype(BF16), s, BF16,
                           out_scale=HEAD_DIM ** -0.5)
            o = _attention(q, kv, b, s)
            w_out = b_w_out[j]
        x2 = _proj_residual(o, w_out.astype(BF16), x2, g1, s)
        last = layer == depth - 1
        x2 = _mlp(x2, norm_mlp[layer], sh2, sc2, g2, mlp_w1[layer].astype(BF16),
                  mlp_w2[layer].astype(BF16), final_norm, s, final_norm=last)
        if layer == n_a - 1:
            kv = _norm_proj(x2, kv_norm, kv_mod[:, 0], kv_mod[:, 1], w_kv.astype(BF16), s, BF16)
    return x2.reshape(b, s, d)
```

```python
import functools

import jax
import jax.numpy as jnp
from jax import lax
from jax.experimental import pallas as pl
from jax.experimental.pallas import tpu as pltpu

HEAD_DIM = 128
NORM_EPS = 1e-6
HGRN_CHUNK = 32
HGRN_HEADS_PER_STEP = 4
HGRN_UNROLL = 8
ATTN_Q = 256
ATTN_HEADS = 4
ATTN_HALF = 128
LOG2E = 1.4426950408889634
VMEM_LIMIT_BYTES = 56 * 1024 * 1024

F32 = jnp.float32
BF16 = jnp.bfloat16


def _tile(n, pref):
    t = min(n, pref)
    while n % t:
        t -= 1
    return t


def _params(*sem):
    return pltpu.CompilerParams(dimension_semantics=sem, vmem_limit_bytes=VMEM_LIMIT_BYTES)


def _split_bf16(a):
    hi = a.astype(BF16)
    lo = (a - hi.astype(F32)).astype(BF16)
    return hi, lo


def _rms_mod(x, gain, shift, scale):
    inv = lax.rsqrt(jnp.mean(x * x, axis=-1, keepdims=True) + NORM_EPS)
    return (x * inv * gain) * (1.0 + scale) + shift


def _cond_kernel(c_ref, w_ref, b_ref, o_ref):
    c = c_ref[...]
    c_act = c * (1.0 / (1.0 + jnp.exp(-c)))
    c_hi, c_lo = _split_bf16(c_act)
    w_hi, w_lo = _split_bf16(w_ref[...])
    lhs = jnp.concatenate([c_hi, c_lo], axis=0)
    acc = jnp.dot(lhs, w_hi, preferred_element_type=F32)
    nb = c.shape[0]
    out = acc[:nb] + acc[nb:] + jnp.dot(c_hi, w_lo, preferred_element_type=F32)
    o_ref[...] = out + b_ref[...]


def _conditioning(c, w, b):
    nl, d, n = w.shape
    nb = c.shape[0]
    tn = _tile(n, 1024)
    return pl.pallas_call(
        _cond_kernel,
        grid=(nl, n // tn),
        in_specs=[
            pl.BlockSpec((nb, d), lambda l, j: (0, 0)),
            pl.BlockSpec((None, d, tn), lambda l, j: (l, 0, j)),
            pl.BlockSpec((None, 1, tn), lambda l, j: (l, 0, j)),
        ],
        out_specs=pl.BlockSpec((None, nb, tn), lambda l, j: (l, 0, j)),
        out_shape=jax.ShapeDtypeStruct((nl, nb, n), F32),
        compiler_params=_params("parallel", "parallel"),
        name="conditioning",
    )(c, w, b.reshape(nl, 1, n))


def _norm_proj_kernel(x_ref, gain_ref, shift_ref, scale_ref, w_ref, o_ref, h_ref, *, out_scale):
    @pl.when(pl.program_id(1) == 0)
    def _():
        h = _rms_mod(x_ref[...], gain_ref[...], shift_ref[0], scale_ref[0])
        h_ref[...] = h.astype(BF16)

    acc = jnp.dot(h_ref[...], w_ref[...], preferred_element_type=F32)
    if out_scale != 1.0:
        acc = acc * out_scale
    o_ref[...] = acc.astype(o_ref.dtype)


def _norm_proj(x2, gain, shift, scale, w, seq, out_dtype, out_scale=1.0):
    t, d = x2.shape
    n = w.shape[1]
    tm = _tile(seq, 1024)
    tn = _tile(n, 1024)
    per_b = seq // tm
    return pl.pallas_call(
        functools.partial(_norm_proj_kernel, out_scale=out_scale),
        grid=(t // tm, n // tn),
        in_specs=[
            pl.BlockSpec((tm, d), lambda i, j: (i, 0)),
            pl.BlockSpec((1, d), lambda i, j: (0, 0)),
            pl.BlockSpec((1, 1, d), lambda i, j: (i // per_b, 0, 0)),
            pl.BlockSpec((1, 1, d), lambda i, j: (i // per_b, 0, 0)),
            pl.BlockSpec((d, tn), lambda i, j: (0, j)),
        ],
        out_specs=pl.BlockSpec((tm, tn), lambda i, j: (i, j)),
        out_shape=jax.ShapeDtypeStruct((t, n), out_dtype),
        scratch_shapes=[pltpu.VMEM((tm, d), BF16)],
        compiler_params=_params("parallel", "arbitrary"),
        name="norm_proj",
    )(x2, gain.reshape(1, d), shift, scale, w)


def _proj_residual_kernel(a_ref, w_ref, x_ref, gate_ref, o_ref):
    y = jnp.dot(a_ref[...], w_ref[...], preferred_element_type=F32)
    o_ref[...] = x_ref[...] + gate_ref[0] * y


def _proj_residual(a, w, x2, gate, seq):
    t, k = a.shape
    d = w.shape[1]
    tm = _tile(seq, 512)
    per_b = seq // tm
    return pl.pallas_call(
        _proj_residual_kernel,
        grid=(t // tm,),
        in_specs=[
            pl.BlockSpec((tm, k), lambda i: (i, 0)),
            pl.BlockSpec((k, d), lambda i: (0, 0)),
            pl.BlockSpec((tm, d), lambda i: (i, 0)),
            pl.BlockSpec((1, 1, d), lambda i: (i // per_b, 0, 0)),
        ],
        out_specs=pl.BlockSpec((tm, d), lambda i: (i, 0)),
        out_shape=jax.ShapeDtypeStruct((t, d), F32),
        compiler_params=_params("parallel"),
        name="proj_residual",
    )(a, w, x2, gate)


def _mlp_kernel(x_ref, gain_ref, shift_ref, scale_ref, gate_ref, w1_ref, w2_ref, fgain_ref,
                o_ref, h_ref, acc_ref, *, final_norm):
    j = pl.program_id(1)

    @pl.when(j == 0)
    def _():
        h = _rms_mod(x_ref[...], gain_ref[...], shift_ref[0], scale_ref[0])
        h_ref[...] = h.astype(BF16)
        acc_ref[...] = jnp.zeros_like(acc_ref)

    a = jnp.dot(h_ref[...], w1_ref[...], preferred_element_type=F32)
    a = jnp.maximum(a, 0.0)
    a = (a * a).astype(BF16)
    acc_ref[...] += jnp.dot(a, w2_ref[...], preferred_element_type=F32)

    @pl.when(j == pl.num_programs(1) - 1)
    def _():
        y = x_ref[...] + gate_ref[0] * acc_ref[...]
        if final_norm:
            inv = lax.rsqrt(jnp.mean(y * y, axis=-1, keepdims=True) + NORM_EPS)
            y = y * inv * fgain_ref[...]
        o_ref[...] = y


def _mlp(x2, gain, shift, scale, gate, w1, w2, fgain, seq, final_norm):
    t, d = x2.shape
    f = w1.shape[1]
    tm = _tile(seq, 512)
    tf = _tile(f, 1024)
    per_b = seq // tm
    return pl.pallas_call(
        functools.partial(_mlp_kernel, final_norm=final_norm),
        grid=(t // tm, f // tf),
        in_specs=[
            pl.BlockSpec((tm, d), lambda i, j: (i, 0)),
            pl.BlockSpec((1, d), lambda i, j: (0, 0)),
            pl.BlockSpec((1, 1, d), lambda i, j: (i // per_b, 0, 0)),
            pl.BlockSpec((1, 1, d), lambda i, j: (i // per_b, 0, 0)),
            pl.BlockSpec((1, 1, d), lambda i, j: (i // per_b, 0, 0)),
            pl.BlockSpec((d, tf), lambda i, j: (0, j)),
            pl.BlockSpec((tf, d), lambda i, j: (j, 0)),
            pl.BlockSpec((1, d), lambda i, j: (0, 0)),
        ],
        out_specs=pl.BlockSpec((tm, d), lambda i, j: (i, 0)),
        out_shape=jax.ShapeDtypeStruct((t, d), F32),
        scratch_shapes=[pltpu.VMEM((tm, d), BF16), pltpu.VMEM((tm, d), F32)],
        compiler_params=_params("parallel", "arbitrary"),
        name="mlp",
    )(x2, gain.reshape(1, d), shift, scale, gate, w1, w2, fgain.reshape(1, d))


def _silu(x):
    h = 0.5 * x
    return h + h * jnp.tanh(h)


def _hgrn_kernel(q_ref, f_ref, i_ref, g_ref, lbl_ref, og_ref, o_ref, st_ref, *,
                 heads, chunk, n_chunks, layer):
    hd = HEAD_DIM

    @pl.when(pl.program_id(2) == 0)
    def _():
        st_ref[...] = jnp.zeros_like(st_ref)

    logits = lbl_ref[...]
    ex = jnp.exp(logits - jnp.max(logits, axis=0, keepdims=True))
    lb = jnp.sum(ex[:layer + 1], axis=0, keepdims=True) / jnp.sum(ex, axis=0, keepdims=True)
    half_gap = 0.5 * (1.0 - lb)
    f_mid = lb + half_gap
    og = og_ref[...]
    row = lax.broadcasted_iota(jnp.int32, (chunk, heads * hd), 0)
    ti = lax.broadcasted_iota(jnp.int32, (chunk, chunk), 0)
    si = lax.broadcasted_iota(jnp.int32, (chunk, chunk), 1)
    causal = si <= ti

    def body(c, carry):
        r0 = pl.multiple_of(c * chunk, chunk)
        rows = pl.ds(r0, chunk)
        swing = half_gap * jnp.tanh(0.5 * f_ref[rows, :])
        log2_f = jnp.log(f_mid + swing) * LOG2E
        k = half_gap - swing
        cum = log2_f
        d = 1
        while d < chunk:
            cum = cum + jnp.where(row >= d, pltpu.roll(cum, d, 0), 0.0)
            d *= 2
        mid = cum[chunk // 2 - 1:chunk // 2, :]
        tot = cum[chunk - 1:chunk, :]
        qs = _silu(q_ref[rows, :])
        q_in = (qs * jnp.exp2(cum - mid)).astype(BF16)
        k_in = (k * jnp.exp2(mid - cum)).astype(BF16)
        q_dec = (qs * jnp.exp2(cum)).astype(BF16)
        k_st = (k * jnp.exp2(tot - cum)).astype(BF16)
        dec = jnp.exp2(tot)
        v = i_ref[rows, :].astype(BF16)
        gs = _silu(g_ref[rows, :])
        for h in range(heads):
            cs = slice(h * hd, (h + 1) * hd)
            sc = lax.dot_general(q_in[:, cs], k_in[:, cs], (((1,), (1,)), ((), ())),
                                 preferred_element_type=F32)
            sc = jnp.where(causal, sc, 0.0).astype(BF16)
            st = st_ref[h]
            o = jnp.dot(sc, v[:, cs], preferred_element_type=F32)
            o = o + lax.dot_general(q_dec[:, cs], st.astype(BF16), (((1,), (1,)), ((), ())),
                                    preferred_element_type=F32)
            upd = lax.dot_general(v[:, cs], k_st[:, cs], (((0,), (0,)), ((), ())),
                                  preferred_element_type=F32)
            st_ref[h] = dec[:, cs] * st + upd
            inv = lax.rsqrt(jnp.mean(o * o, axis=-1, keepdims=True) + NORM_EPS)
            o_ref[rows, cs] = (o * inv * og[:, cs] * gs[:, cs]).astype(o_ref.dtype)
        return carry

    lax.fori_loop(0, n_chunks, body, 0, unroll=HGRN_UNROLL)


def _hgrn(proj, lb_logits, layer, out_gain, batch, seq):
    t, d4 = proj.shape
    d = d4 // 4
    slots = lb_logits.shape[0]
    hb = HGRN_HEADS_PER_STEP
    w = hb * HEAD_DIM
    groups = d // w
    tc = _tile(seq, 512)
    nt = seq // tc

    def part(p):
        return pl.BlockSpec((tc, w), lambda b, hg, s: (b * nt + s, p * groups + hg))

    vec = pl.BlockSpec((1, w), lambda b, hg, s: (0, hg))
    slot_vec = pl.BlockSpec((slots, w), lambda b, hg, s: (0, hg))
    return pl.pallas_call(
        functools.partial(_hgrn_kernel, heads=hb, chunk=HGRN_CHUNK, n_chunks=tc // HGRN_CHUNK,
                          layer=layer),
        grid=(batch, groups, nt),
        in_specs=[part(0), part(1), part(2), part(3), slot_vec, vec],
        out_specs=pl.BlockSpec((tc, w), lambda b, hg, s: (b * nt + s, hg)),
        out_shape=jax.ShapeDtypeStruct((t, d), BF16),
        scratch_shapes=[pltpu.VMEM((hb, HEAD_DIM, HEAD_DIM), F32)],
        compiler_params=_params("parallel", "parallel", "arbitrary"),
        name="hgrn",
    )(proj, proj, proj, proj, lb_logits.astype(F32), out_gain.reshape(1, d))


def _softplus2(z):
    m = jnp.maximum(z, 0.0)
    return m + jnp.log(1.0 + jnp.exp2(z - m - m)) * LOG2E


def _attn_kernel(q_ref, k_ref, v_ref, sel_ref, o_ref, vt_ref, z0, z1, zk0, zk1, hl0, hl1, w0, w1, acc_ref,
                 *, heads, tq, half):
    z_ref, zk_ref, hl_ref, w_ref = (z0, z1), (zk0, zk1), (hl0, hl1), (w0, w1)
    hd = HEAD_DIM
    lanes = heads * tq
    n_half = tq // half
    assert n_half == 2, "the two-slot pipeline pairs the two masked blocks of a query tile"
    qi = pl.program_id(3)

    @pl.when((pl.program_id(2) == 0) & (qi == 0))
    def _():
        def transpose_block(c, carry):
            rs = pl.ds(pl.multiple_of(c * hd, hd), hd)
            vt_ref[c] = v_ref[rs, :].astype(F32).T.astype(BF16)
            return carry
        lax.fori_loop(0, vt_ref.shape[0], transpose_block, 0)

    t_in = lax.broadcasted_iota(jnp.int32, (half, tq), 1)
    s_in = lax.broadcasted_iota(jnp.int32, (half, tq), 0)
    last_kb = n_half * qi + (n_half - 1)
    cols = [slice(g * tq, (g + 1) * tq) for g in range(heads)]

    def kb_of(n):
        return jnp.maximum(last_kb - n, 0)

    def scores(n, slot, g):
        ks = pl.ds(pl.multiple_of(kb_of(n) * half, half), half)
        z_ref[slot][:, cols[g]] = lax.dot_general(
            k_ref[ks, :], q_ref[:, g * hd:(g + 1) * hd], (((1,), (1,)), ((), ())),
            preferred_element_type=F32)

    def split(slot, g, masked_n):
        z = z_ref[slot][:, cols[g]]
        sp = _softplus2(z)
        if masked_n is not None:
            sp = jnp.where((s_in + (n_half - 1 - masked_n) * half) < t_in, sp, 0.0)
        hi, lo = _split_bf16(sp)
        hl_ref[slot][:half, cols[g]] = hi
        hl_ref[slot][half:, cols[g]] = lo
        zk_ref[slot][:, cols[g]] = z

    def suffix(slot, g):
        return jnp.dot(sel_ref[...], hl_ref[slot][:, cols[g]], preferred_element_type=F32)

    def weights(slot, g, cs, run, masked_n):
        w = jnp.exp2(zk_ref[slot][:, cols[g]] + cs + run)
        if masked_n is not None:
            w = jnp.where((s_in + (n_half - 1 - masked_n) * half) < t_in, w, 0.0)
        w_ref[slot][:, cols[g]] = w.astype(BF16)
        return run + cs[0:1, :]

    def values(n, slot, g):
        acc_ref[:, cols[g]] += jnp.dot(vt_ref[kb_of(n)], w_ref[slot][:, cols[g]],
                                       preferred_element_type=F32)

    def iteration(i, slot, runs, masked_split, masked_weights, with_values=True):
        other = 1 - slot
        hs = range(heads)
        cs = [suffix(slot, g) for g in hs]
        if with_values:
            for g in hs:
                values(i - 1, other, g)
        for g in hs:
            split(other, g, masked_split)
        out = [weights(slot, g, cs[g], runs[g], masked_weights) for g in hs]
        for g in hs:
            scores(i + 2, slot, g)
        return out

    acc_ref[...] = jnp.zeros_like(acc_ref)
    for g in range(heads):
        scores(0, 0, g)
        scores(1, 1, g)
    for g in range(heads):
        split(0, g, 0)
    runs = [jnp.zeros((1, tq), F32) for _ in range(heads)]
    runs = iteration(0, 0, runs, 1, 0, with_values=False)
    runs = iteration(1, 1, runs, None, 1)

    def body(j, runs):
        i = 2 + 2 * j
        runs = iteration(i, 0, list(runs), None, None)
        return tuple(iteration(i + 1, 1, runs, None, None))

    lax.fori_loop(0, qi, body, tuple(runs))
    for g in range(heads):
        values(n_half * qi + 1, 1, g)
    for g in range(heads):
        o_ref[:, g * hd:(g + 1) * hd] = acc_ref[:, cols[g]].T.astype(o_ref.dtype)


def _attention(q, kv, batch, seq):
    t, d = q.shape
    hd = HEAD_DIM
    kvh = kv.shape[1] // (2 * hd)
    group = d // hd // kvh
    heads = _tile(group, ATTN_HEADS)
    tq = _tile(seq, ATTN_Q)
    half = _tile(tq, ATTN_HALF)
    nq = seq // tq
    gsteps = group // heads
    si = lax.broadcasted_iota(jnp.int32, (half, half), 0)
    ji = lax.broadcasted_iota(jnp.int32, (half, half), 1)
    neg_suffix = -(ji >= si).astype(BF16)
    sel = jnp.concatenate([neg_suffix, neg_suffix], axis=1)
    return pl.pallas_call(
        functools.partial(_attn_kernel, heads=heads, tq=tq, half=half),
        grid=(batch, kvh, gsteps, nq),
        in_specs=[
            pl.BlockSpec((tq, heads * hd), lambda b, h, g, i: (b * nq + i, h * gsteps + g)),
            pl.BlockSpec((seq, hd), lambda b, h, g, i: (b, h)),
            pl.BlockSpec((seq, hd), lambda b, h, g, i: (b, kvh + h)),
            pl.BlockSpec((half, 2 * half), lambda b, h, g, i: (0, 0)),
        ],
        out_specs=pl.BlockSpec((tq, heads * hd), lambda b, h, g, i: (b * nq + i, h * gsteps + g)),
        out_shape=jax.ShapeDtypeStruct((t, d), BF16),
        scratch_shapes=[
            pltpu.VMEM((seq // hd, hd, hd), BF16),
            pltpu.VMEM((half, heads * tq), F32),
            pltpu.VMEM((half, heads * tq), F32),
            pltpu.VMEM((half, heads * tq), F32),
            pltpu.VMEM((half, heads * tq), F32),
            pltpu.VMEM((2 * half, heads * tq), BF16),
            pltpu.VMEM((2 * half, heads * tq), BF16),
            pltpu.VMEM((half, heads * tq), BF16),
            pltpu.VMEM((half, heads * tq), BF16),
            pltpu.VMEM((hd, heads * tq), F32),
        ],
        compiler_params=_params("parallel", "parallel", "arbitrary", "arbitrary"),
        name="stick_breaking",
    )(q, kv, kv, sel)


def kernel(x, c, ada_w, ada_b, norm_mix, norm_mlp, a_w_in, a_lb_logits, a_out_gain, a_w_out,
           kv_ada_w, kv_ada_b, kv_norm, w_kv, b_w_q, b_w_out, mlp_w1, mlp_w2, final_norm):
    b, s, d = x.shape
    depth = ada_w.shape[0]
    n_a = a_w_in.shape[0]
    x2 = x.reshape(b * s, d)

    mod = _conditioning(c, ada_w, ada_b).reshape(depth, b, 6, 1, d)
    kv_mod = _conditioning(c, kv_ada_w[None], kv_ada_b[None]).reshape(b, 2, 1, d)

    kv = None
    for layer in range(depth):
        sh1, sc1, g1, sh2, sc2, g2 = (mod[layer, :, p] for p in range(6))
        if layer < n_a:
            proj = _norm_proj(x2, norm_mix[layer], sh1, sc1, a_w_in[layer].astype(BF16), s, F32)
            o = _hgrn(proj, a_lb_logits, layer, a_out_gain[layer], b, s)
            w_out = a_w_out[layer]
        else:
            j = layer - n_a
            q = _norm_proj(x2, norm_mix[layer], sh1, sc1, b_w_q[j].astype(BF16), s, BF16,
                           out_scale=LOG2E * HEAD_DIM ** -0.5)
            o = _attention(q, kv, b, s)
            w_out = b_w_out[j]
        x2 = _proj_residual(o, w_out.astype(BF16), x2, g1, s)
        last = layer == depth - 1
        x2 = _mlp(x2, norm_mlp[layer], sh2, sc2, g2, mlp_w1[layer].astype(BF16),
                  mlp_w2[layer].astype(BF16), final_norm, s, final_norm=last)
        if layer == n_a - 1:
            kv = _norm_proj(x2, kv_norm, kv_mod[:, 0], kv_mod[:, 1], w_kv.astype(BF16), s, BF16)
    return x2.reshape(b, s, d)
```

```python
import functools

import jax
import jax.numpy as jnp
from jax import lax
from jax.experimental import pallas as pl
from jax.experimental.pallas import tpu as pltpu

HEAD_DIM = 128
NORM_EPS = 1e-6
HGRN_CHUNK = 32
HGRN_HEADS_PER_STEP = 4
HGRN_UNROLL = 8
ATTN_Q = 256
ATTN_HEADS = 4
ATTN_HALF = 128
ATTN_DEAD_LOG2 = -160.0
LOG2E = 1.4426950408889634
VMEM_LIMIT_BYTES = 56 * 1024 * 1024

F32 = jnp.float32
BF16 = jnp.bfloat16


def _tile(n, pref):
    t = min(n, pref)
    while n % t:
        t -= 1
    return t


def _params(*sem):
    return pltpu.CompilerParams(dimension_semantics=sem, vmem_limit_bytes=VMEM_LIMIT_BYTES)


def _split_bf16(a):
    hi = a.astype(BF16)
    lo = (a - hi.astype(F32)).astype(BF16)
    return hi, lo


def _rms_mod(x, gain, shift, scale):
    inv = lax.rsqrt(jnp.mean(x * x, axis=-1, keepdims=True) + NORM_EPS)
    return (x * inv) * (gain * (1.0 + scale)) + shift


def _cond_kernel(c_ref, w_ref, b_ref, o_ref):
    c = c_ref[...]
    c_act = c * (1.0 / (1.0 + jnp.exp(-c)))
    c_hi, c_lo = _split_bf16(c_act)
    w_hi, w_lo = _split_bf16(w_ref[...])
    lhs = jnp.concatenate([c_hi, c_lo], axis=0)
    acc = jnp.dot(lhs, w_hi, preferred_element_type=F32)
    nb = c.shape[0]
    out = acc[:nb] + acc[nb:] + jnp.dot(c_hi, w_lo, preferred_element_type=F32)
    o_ref[...] = out + b_ref[...]


def _conditioning(c, w, b):
    nl, d, n = w.shape
    nb = c.shape[0]
    tn = _tile(n, 1024)
    return pl.pallas_call(
        _cond_kernel,
        grid=(nl, n // tn),
        in_specs=[
            pl.BlockSpec((nb, d), lambda l, j: (0, 0)),
            pl.BlockSpec((None, d, tn), lambda l, j: (l, 0, j)),
            pl.BlockSpec((None, 1, tn), lambda l, j: (l, 0, j)),
        ],
        out_specs=pl.BlockSpec((None, nb, tn), lambda l, j: (l, 0, j)),
        out_shape=jax.ShapeDtypeStruct((nl, nb, n), F32),
        compiler_params=_params("parallel", "parallel"),
        name="conditioning",
    )(c, w, b.reshape(nl, 1, n))


def _norm_proj_kernel(x_ref, gain_ref, shift_ref, scale_ref, w_ref, o_ref, h_ref, *, out_scale):
    @pl.when(pl.program_id(1) == 0)
    def _():
        h = _rms_mod(x_ref[...], gain_ref[...], shift_ref[0], scale_ref[0])
        h_ref[...] = h.astype(BF16)

    acc = jnp.dot(h_ref[...], w_ref[...], preferred_element_type=F32)
    if out_scale != 1.0:
        acc = acc * out_scale
    o_ref[...] = acc.astype(o_ref.dtype)


def _norm_proj(x2, gain, shift, scale, w, seq, out_dtype, out_scale=1.0):
    t, d = x2.shape
    n = w.shape[1]
    tm = _tile(seq, 1024)
    tn = _tile(n, 1024)
    per_b = seq // tm
    return pl.pallas_call(
        functools.partial(_norm_proj_kernel, out_scale=out_scale),
        grid=(t // tm, n // tn),
        in_specs=[
            pl.BlockSpec((tm, d), lambda i, j: (i, 0)),
            pl.BlockSpec((1, d), lambda i, j: (0, 0)),
            pl.BlockSpec((1, 1, d), lambda i, j: (i // per_b, 0, 0)),
            pl.BlockSpec((1, 1, d), lambda i, j: (i // per_b, 0, 0)),
            pl.BlockSpec((d, tn), lambda i, j: (0, j)),
        ],
        out_specs=pl.BlockSpec((tm, tn), lambda i, j: (i, j)),
        out_shape=jax.ShapeDtypeStruct((t, n), out_dtype),
        scratch_shapes=[pltpu.VMEM((tm, d), BF16)],
        compiler_params=_params("parallel", "arbitrary"),
        name="norm_proj",
    )(x2, gain.reshape(1, d), shift, scale, w)


def _proj_residual_kernel(a_ref, w_ref, x_ref, gate_ref, o_ref):
    y = jnp.dot(a_ref[...], w_ref[...], preferred_element_type=F32)
    o_ref[...] = x_ref[...] + gate_ref[0] * y


def _proj_residual(a, w, x2, gate, seq):
    t, k = a.shape
    d = w.shape[1]
    tm = _tile(seq, 512)
    per_b = seq // tm
    return pl.pallas_call(
        _proj_residual_kernel,
        grid=(t // tm,),
        in_specs=[
            pl.BlockSpec((tm, k), lambda i: (i, 0)),
            pl.BlockSpec((k, d), lambda i: (0, 0)),
            pl.BlockSpec((tm, d), lambda i: (i, 0)),
            pl.BlockSpec((1, 1, d), lambda i: (i // per_b, 0, 0)),
        ],
        out_specs=pl.BlockSpec((tm, d), lambda i: (i, 0)),
        out_shape=jax.ShapeDtypeStruct((t, d), F32),
        compiler_params=_params("parallel"),
        name="proj_residual",
    )(a, w, x2, gate)


def _mlp_kernel(x_ref, gain_ref, shift_ref, scale_ref, gate_ref, w1_ref, w2_ref, fgain_ref,
                o_ref, h_ref, acc_ref, *, final_norm):
    j = pl.program_id(1)

    @pl.when(j == 0)
    def _():
        h = _rms_mod(x_ref[...], gain_ref[...], shift_ref[0], scale_ref[0])
        h_ref[...] = h.astype(BF16)
        acc_ref[...] = jnp.zeros_like(acc_ref)

    a = jnp.dot(h_ref[...], w1_ref[...], preferred_element_type=F32)
    a = jnp.maximum(a, 0.0)
    a = (a * a).astype(BF16)
    acc_ref[...] += jnp.dot(a, w2_ref[...], preferred_element_type=F32)

    @pl.when(j == pl.num_programs(1) - 1)
    def _():
        y = x_ref[...] + gate_ref[0] * acc_ref[...]
        if final_norm:
            inv = lax.rsqrt(jnp.mean(y * y, axis=-1, keepdims=True) + NORM_EPS)
            y = y * inv * fgain_ref[...]
        o_ref[...] = y


def _mlp(x2, gain, shift, scale, gate, w1, w2, fgain, seq, final_norm):
    t, d = x2.shape
    f = w1.shape[1]
    tm = _tile(seq, 512)
    tf = _tile(f, 1024)
    per_b = seq // tm
    return pl.pallas_call(
        functools.partial(_mlp_kernel, final_norm=final_norm),
        grid=(t // tm, f // tf),
        in_specs=[
            pl.BlockSpec((tm, d), lambda i, j: (i, 0)),
            pl.BlockSpec((1, d), lambda i, j: (0, 0)),
            pl.BlockSpec((1, 1, d), lambda i, j: (i // per_b, 0, 0)),
            pl.BlockSpec((1, 1, d), lambda i, j: (i // per_b, 0, 0)),
            pl.BlockSpec((1, 1, d), lambda i, j: (i // per_b, 0, 0)),
            pl.BlockSpec((d, tf), lambda i, j: (0, j)),
            pl.BlockSpec((tf, d), lambda i, j: (j, 0)),
            pl.BlockSpec((1, d), lambda i, j: (0, 0)),
        ],
        out_specs=pl.BlockSpec((tm, d), lambda i, j: (i, 0)),
        out_shape=jax.ShapeDtypeStruct((t, d), F32),
        scratch_shapes=[pltpu.VMEM((tm, d), BF16), pltpu.VMEM((tm, d), F32)],
        compiler_params=_params("parallel", "arbitrary"),
        name="mlp",
    )(x2, gain.reshape(1, d), shift, scale, gate, w1, w2, fgain.reshape(1, d))


def _silu(x):
    h = 0.5 * x
    return h + h * jnp.tanh(h)


def _hgrn_kernel(q_ref, f_ref, i_ref, g_ref, lbl_ref, og_ref, o_ref, st_ref, *,
                 heads, chunk, n_chunks, layer):
    hd = HEAD_DIM

    @pl.when(pl.program_id(2) == 0)
    def _():
        st_ref[...] = jnp.zeros_like(st_ref)

    logits = lbl_ref[...]
    ex = jnp.exp(logits - jnp.max(logits, axis=0, keepdims=True))
    lb = jnp.sum(ex[:layer + 1], axis=0, keepdims=True) / jnp.sum(ex, axis=0, keepdims=True)
    half_gap = 0.5 * (1.0 - lb)
    f_mid = lb + half_gap
    og = og_ref[...]
    row = lax.broadcasted_iota(jnp.int32, (chunk, heads * hd), 0)
    ti = lax.broadcasted_iota(jnp.int32, (chunk, chunk), 0)
    si = lax.broadcasted_iota(jnp.int32, (chunk, chunk), 1)
    causal = si <= ti

    def body(c, carry):
        r0 = pl.multiple_of(c * chunk, chunk)
        rows = pl.ds(r0, chunk)
        swing = half_gap * jnp.tanh(0.5 * f_ref[rows, :])
        log2_f = jnp.log(f_mid + swing) * LOG2E
        k = half_gap - swing
        cum = log2_f
        d = 1
        while d < chunk:
            cum = cum + jnp.where(row >= d, pltpu.roll(cum, d, 0), 0.0)
            d *= 2
        mid = cum[chunk // 2 - 1:chunk // 2, :]
        tot = cum[chunk - 1:chunk, :]
        qs = _silu(q_ref[rows, :])
        q_in = (qs * jnp.exp2(cum - mid)).astype(BF16)
        k_in = (k * jnp.exp2(mid - cum)).astype(BF16)
        q_dec = (qs * jnp.exp2(cum)).astype(BF16)
        k_st = (k * jnp.exp2(tot - cum)).astype(BF16)
        dec = jnp.exp2(tot)
        v = i_ref[rows, :].astype(BF16)
        gs = _silu(g_ref[rows, :])
        for h in range(heads):
            cs = slice(h * hd, (h + 1) * hd)
            sc = lax.dot_general(q_in[:, cs], k_in[:, cs], (((1,), (1,)), ((), ())),
                                 preferred_element_type=F32)
            sc = jnp.where(causal, sc, 0.0).astype(BF16)
            st = st_ref[h]
            o = jnp.dot(sc, v[:, cs], preferred_element_type=F32)
            o = o + lax.dot_general(q_dec[:, cs], st.astype(BF16), (((1,), (1,)), ((), ())),
                                    preferred_element_type=F32)
            upd = lax.dot_general(v[:, cs], k_st[:, cs], (((0,), (0,)), ((), ())),
                                  preferred_element_type=F32)
            st_ref[h] = dec[:, cs] * st + upd
            inv = lax.rsqrt(jnp.mean(o * o, axis=-1, keepdims=True) + NORM_EPS)
            o_ref[rows, cs] = (o * inv * og[:, cs] * gs[:, cs]).astype(o_ref.dtype)
        return carry

    lax.fori_loop(0, n_chunks, body, 0, unroll=HGRN_UNROLL)


def _hgrn(proj, lb_logits, layer, out_gain, batch, seq):
    t, d4 = proj.shape
    d = d4 // 4
    slots = lb_logits.shape[0]
    hb = HGRN_HEADS_PER_STEP
    w = hb * HEAD_DIM
    groups = d // w
    tc = _tile(seq, 512)
    nt = seq // tc

    def part(p):
        return pl.BlockSpec((tc, w), lambda b, hg, s: (b * nt + s, p * groups + hg))

    vec = pl.BlockSpec((1, w), lambda b, hg, s: (0, hg))
    slot_vec = pl.BlockSpec((slots, w), lambda b, hg, s: (0, hg))
    return pl.pallas_call(
        functools.partial(_hgrn_kernel, heads=hb, chunk=HGRN_CHUNK, n_chunks=tc // HGRN_CHUNK,
                          layer=layer),
        grid=(batch, groups, nt),
        in_specs=[part(0), part(1), part(2), part(3), slot_vec, vec],
        out_specs=pl.BlockSpec((tc, w), lambda b, hg, s: (b * nt + s, hg)),
        out_shape=jax.ShapeDtypeStruct((t, d), BF16),
        scratch_shapes=[pltpu.VMEM((hb, HEAD_DIM, HEAD_DIM), F32)],
        compiler_params=_params("parallel", "parallel", "arbitrary"),
        name="hgrn",
    )(proj, proj, proj, proj, lb_logits.astype(F32), out_gain.reshape(1, d))


def _softplus2(z):
    m = jnp.maximum(z, 0.0)
    return m + jnp.log(1.0 + jnp.exp2(z - m - m)) * LOG2E


def _attn_kernel(q_ref, k_ref, v_ref, sel_ref, o_ref, vt_ref, qt_ref, z0, z1, zk0, zk1, hl0, hl1, w0, w1,
                 acc_ref, *, heads, tq, half):
    z_ref, zk_ref, hl_ref, w_ref = (z0, z1), (zk0, zk1), (hl0, hl1), (w0, w1)
    hd = HEAD_DIM
    lanes = heads * tq
    n_half = tq // half
    assert n_half == 2, "the two-slot pipeline pairs the two masked blocks of a query tile"
    qi = pl.program_id(3)

    @pl.when((pl.program_id(2) == 0) & (qi == 0))
    def _():
        def transpose_block(c, carry):
            rs = pl.ds(pl.multiple_of(c * hd, hd), hd)
            vt_ref[c] = v_ref[rs, :].astype(F32).T.astype(BF16)
            return carry
        lax.fori_loop(0, vt_ref.shape[0], transpose_block, 0)

    t_in = lax.broadcasted_iota(jnp.int32, (half, tq), 1)
    s_in = lax.broadcasted_iota(jnp.int32, (half, tq), 0)
    last_kb = n_half * qi + (n_half - 1)
    cols = [slice(g * tq, (g + 1) * tq) for g in range(heads)]

    for g in range(heads):
        qt_ref[:, cols[g]] = q_ref[:, g * hd:(g + 1) * hd].astype(F32).T.astype(BF16)

    def pair_start(n):
        return jnp.maximum(last_kb - n - 1, 0)

    def scores_pair(n):
        ks = pl.ds(pl.multiple_of(pair_start(n) * half, half), 2 * half)
        zz = jnp.dot(k_ref[ks, :], qt_ref[...], preferred_element_type=F32)
        z_ref[1][...] = zz[:half]
        z_ref[0][...] = zz[half:]

    def split(slot, g, masked_n):
        z = z_ref[slot][:, cols[g]]
        sp = _softplus2(z)
        if masked_n is not None:
            sp = jnp.where((s_in + (n_half - 1 - masked_n) * half) < t_in, sp, 0.0)
        hi, lo = _split_bf16(sp)
        hl_ref[slot][:half, cols[g]] = hi
        hl_ref[slot][half:, cols[g]] = lo
        zk_ref[slot][:, cols[g]] = z

    def suffix(slot, g):
        return jnp.dot(sel_ref[...], hl_ref[slot][:, cols[g]], preferred_element_type=F32)

    def weights(slot, g, cs, run, masked_n):
        w = jnp.exp2(zk_ref[slot][:, cols[g]] + cs + run)
        if masked_n is not None:
            w = jnp.where((s_in + (n_half - 1 - masked_n) * half) < t_in, w, 0.0)
        w_ref[slot][:, cols[g]] = w.astype(BF16)
        return run + cs[0:1, :]

    def values_pair(n):
        first = pair_start(n)
        vt = jnp.concatenate([vt_ref[first], vt_ref[first + 1]], axis=1)
        w = jnp.concatenate([w_ref[1][...], w_ref[0][...]], axis=0)
        acc_ref[...] += jnp.dot(vt, w, preferred_element_type=F32)

    hs = range(heads)

    def even_iteration(i, runs, masked_split, masked_weights, with_values=True):
        for g in hs:
            split(1, g, masked_split)
        cs = [suffix(0, g) for g in hs]
        if with_values:
            values_pair(i - 2)
        scores_pair(i + 2)
        return [weights(0, g, cs[g], runs[g], masked_weights) for g in hs]

    def odd_iteration(runs, masked_weights):
        for g in hs:
            split(0, g, None)
        cs = [suffix(1, g) for g in hs]
        return [weights(1, g, cs[g], runs[g], masked_weights) for g in hs]

    acc_ref[...] = jnp.zeros_like(acc_ref)
    scores_pair(0)
    for g in hs:
        split(0, g, 0)
    runs = [jnp.zeros((1, tq), F32) for _ in hs]
    runs = even_iteration(0, runs, 1, 0, with_values=False)
    runs = odd_iteration(runs, 1)

    def keep_walking(carry):
        j, live, _ = carry
        return jnp.logical_and(j < qi, live > 0)

    def body(carry):
        j, _, runs = carry
        runs = even_iteration(2 + 2 * j, list(runs), None, None)
        runs = odd_iteration(runs, None)
        slowest = functools.reduce(jnp.maximum, runs)
        live = (jnp.max(slowest) > ATTN_DEAD_LOG2).astype(jnp.int32)
        return j + 1, live, tuple(runs)

    pairs_done, _, _ = lax.while_loop(keep_walking, body, (jnp.int32(0), jnp.int32(1), tuple(runs)))
    values_pair(n_half * pairs_done)
    for g in hs:
        o_ref[:, g * hd:(g + 1) * hd] = acc_ref[:, cols[g]].T.astype(o_ref.dtype)


def _attention(q, kv, batch, seq):
    t, d = q.shape
    hd = HEAD_DIM
    kvh = kv.shape[1] // (2 * hd)
    group = d // hd // kvh
    heads = _tile(group, ATTN_HEADS)
    tq = _tile(seq, ATTN_Q)
    half = _tile(tq, ATTN_HALF)
    nq = seq // tq
    gsteps = group // heads
    si = lax.broadcasted_iota(jnp.int32, (half, half), 0)
    ji = lax.broadcasted_iota(jnp.int32, (half, half), 1)
    neg_suffix = -(ji >= si).astype(BF16)
    sel = jnp.concatenate([neg_suffix, neg_suffix], axis=1)
    return pl.pallas_call(
        functools.partial(_attn_kernel, heads=heads, tq=tq, half=half),
        grid=(batch, kvh, gsteps, nq),
        in_specs=[
            pl.BlockSpec((tq, heads * hd), lambda b, h, g, i: (b * nq + i, h * gsteps + g)),
            pl.BlockSpec((seq, hd), lambda b, h, g, i: (b, h)),
            pl.BlockSpec((seq, hd), lambda b, h, g, i: (b, kvh + h)),
            pl.BlockSpec((half, 2 * half), lambda b, h, g, i: (0, 0)),
        ],
        out_specs=pl.BlockSpec((tq, heads * hd), lambda b, h, g, i: (b * nq + i, h * gsteps + g)),
        out_shape=jax.ShapeDtypeStruct((t, d), BF16),
        scratch_shapes=[
            pltpu.VMEM((seq // hd, hd, hd), BF16),
            pltpu.VMEM((hd, heads * tq), BF16),
            pltpu.VMEM((half, heads * tq), F32),
            pltpu.VMEM((half, heads * tq), F32),
            pltpu.VMEM((half, heads * tq), F32),
            pltpu.VMEM((half, heads * tq), F32),
            pltpu.VMEM((2 * half, heads * tq), BF16),
            pltpu.VMEM((2 * half, heads * tq), BF16),
            pltpu.VMEM((half, heads * tq), BF16),
            pltpu.VMEM((half, heads * tq), BF16),
            pltpu.VMEM((hd, heads * tq), F32),
        ],
        compiler_params=_params("parallel", "parallel", "arbitrary", "arbitrary"),
        name="stick_breaking",
    )(q, kv, kv, sel)


def kernel(x, c, ada_w, ada_b, norm_mix, norm_mlp, a_w_in, a_lb_logits, a_out_gain, a_w_out,
           kv_ada_w, kv_ada_b, kv_norm, w_kv, b_w_q, b_w_out, mlp_w1, mlp_w2, final_norm):
    b, s, d = x.shape
    depth = ada_w.shape[0]
    n_a = a_w_in.shape[0]
    x2 = x.reshape(b * s, d)

    mod = _conditioning(c, ada_w, ada_b).reshape(depth, b, 6, 1, d)
    kv_mod = _conditioning(c, kv_ada_w[None], kv_ada_b[None]).reshape(b, 2, 1, d)

    kv = None
    for layer in range(depth):
        sh1, sc1, g1, sh2, sc2, g2 = (mod[layer, :, p] for p in range(6))
        if layer < n_a:
            proj = _norm_proj(x2, norm_mix[layer], sh1, sc1, a_w_in[layer].astype(BF16), s, F32)
            o = _hgrn(proj, a_lb_logits, layer, a_out_gain[layer], b, s)
            w_out = a_w_out[layer]
        else:
            j = layer - n_a
            q = _norm_proj(x2, norm_mix[layer], sh1, sc1, b_w_q[j].astype(BF16), s, BF16,
                           out_scale=LOG2E * HEAD_DIM ** -0.5)
            o = _attention(q, kv, b, s)
            w_out = b_w_out[j]
        x2 = _proj_residual(o, w_out.astype(BF16), x2, g1, s)
        last = layer == depth - 1
        x2 = _mlp(x2, norm_mlp[layer], sh2, sc2, g2, mlp_w1[layer].astype(BF16),
                  mlp_w2[layer].astype(BF16), final_norm, s, final_norm=last)
        if layer == n_a - 1:
            kv = _norm_proj(x2, kv_norm, kv_mod[:, 0], kv_mod[:, 1], w_kv.astype(BF16), s, BF16)
    return x2.reshape(b, s, d)
```

```python
import functools

import jax
import jax.numpy as jnp
from jax import lax
from jax.experimental import pallas as pl
from jax.experimental.pallas import tpu as pltpu

HEAD_DIM = 128
NORM_EPS = 1e-6
HGRN_CHUNK = 32
HGRN_HEADS_PER_STEP = 4
HGRN_UNROLL = 8
ATTN_Q = 256
ATTN_HEADS = 4
ATTN_HALF = 128
ATTN_TILES = 2
ATTN_DEAD_LOG2 = -160.0
ATTN_EXHAUSTED_LOG2 = -1e30
LOG2E = 1.4426950408889634
VMEM_LIMIT_BYTES = 58 * 1024 * 1024

F32 = jnp.float32
BF16 = jnp.bfloat16


def _tile(n, pref):
    t = min(n, pref)
    while n % t:
        t -= 1
    return t


def _params(*sem):
    return pltpu.CompilerParams(dimension_semantics=sem, vmem_limit_bytes=VMEM_LIMIT_BYTES)


def _split_bf16(a):
    hi = a.astype(BF16)
    lo = (a - hi.astype(F32)).astype(BF16)
    return hi, lo


def _rms_mod(x, gain, shift, scale):
    inv = lax.rsqrt(jnp.mean(x * x, axis=-1, keepdims=True) + NORM_EPS)
    return (x * inv) * (gain * (1.0 + scale)) + shift


def _cond_kernel(c_ref, w_ref, b_ref, o_ref):
    c = c_ref[...]
    c_act = c * (1.0 / (1.0 + jnp.exp(-c)))
    c_hi, c_lo = _split_bf16(c_act)
    w_hi, w_lo = _split_bf16(w_ref[...])
    lhs = jnp.concatenate([c_hi, c_lo], axis=0)
    acc = jnp.dot(lhs, w_hi, preferred_element_type=F32)
    nb = c.shape[0]
    out = acc[:nb] + acc[nb:] + jnp.dot(c_hi, w_lo, preferred_element_type=F32)
    o_ref[...] = out + b_ref[...]


def _conditioning(c, w, b):
    nl, d, n = w.shape
    nb = c.shape[0]
    tn = _tile(n, 1024)
    return pl.pallas_call(
        _cond_kernel,
        grid=(nl, n // tn),
        in_specs=[
            pl.BlockSpec((nb, d), lambda l, j: (0, 0)),
            pl.BlockSpec((None, d, tn), lambda l, j: (l, 0, j)),
            pl.BlockSpec((None, 1, tn), lambda l, j: (l, 0, j)),
        ],
        out_specs=pl.BlockSpec((None, nb, tn), lambda l, j: (l, 0, j)),
        out_shape=jax.ShapeDtypeStruct((nl, nb, n), F32),
        compiler_params=_params("parallel", "parallel"),
        name="conditioning",
    )(c, w, b.reshape(nl, 1, n))


def _norm_proj_kernel(x_ref, gain_ref, shift_ref, scale_ref, w_ref, o_ref, h_ref, *, out_scale):
    @pl.when(pl.program_id(1) == 0)
    def _():
        h = _rms_mod(x_ref[...], gain_ref[...], shift_ref[0], scale_ref[0])
        h_ref[...] = h.astype(BF16)

    acc = jnp.dot(h_ref[...], w_ref[...], preferred_element_type=F32)
    if out_scale != 1.0:
        acc = acc * out_scale
    o_ref[...] = acc.astype(o_ref.dtype)


def _norm_proj(x2, gain, shift, scale, w, seq, out_dtype, out_scale=1.0):
    t, d = x2.shape
    n = w.shape[1]
    tm = _tile(seq, 1024)
    tn = _tile(n, 2048)
    per_b = seq // tm
    return pl.pallas_call(
        functools.partial(_norm_proj_kernel, out_scale=out_scale),
        grid=(t // tm, n // tn),
        in_specs=[
            pl.BlockSpec((tm, d), lambda i, j: (i, 0)),
            pl.BlockSpec((1, d), lambda i, j: (0, 0)),
            pl.BlockSpec((1, 1, d), lambda i, j: (i // per_b, 0, 0)),
            pl.BlockSpec((1, 1, d), lambda i, j: (i // per_b, 0, 0)),
            pl.BlockSpec((d, tn), lambda i, j: (0, j)),
        ],
        out_specs=pl.BlockSpec((tm, tn), lambda i, j: (i, j)),
        out_shape=jax.ShapeDtypeStruct((t, n), out_dtype),
        scratch_shapes=[pltpu.VMEM((tm, d), BF16)],
        compiler_params=_params("parallel", "arbitrary"),
        name="norm_proj",
    )(x2, gain.reshape(1, d), shift, scale, w)


def _proj_residual_kernel(a_ref, w_ref, x_ref, gate_ref, o_ref):
    y = jnp.dot(a_ref[...], w_ref[...], preferred_element_type=F32)
    o_ref[...] = x_ref[...] + gate_ref[0] * y


def _proj_residual(a, w, x2, gate, seq):
    t, k = a.shape
    d = w.shape[1]
    tm = _tile(seq, 512)
    per_b = seq // tm
    return pl.pallas_call(
        _proj_residual_kernel,
        grid=(t // tm,),
        in_specs=[
            pl.BlockSpec((tm, k), lambda i: (i, 0)),
            pl.BlockSpec((k, d), lambda i: (0, 0)),
            pl.BlockSpec((tm, d), lambda i: (i, 0)),
            pl.BlockSpec((1, 1, d), lambda i: (i // per_b, 0, 0)),
        ],
        out_specs=pl.BlockSpec((tm, d), lambda i: (i, 0)),
        out_shape=jax.ShapeDtypeStruct((t, d), F32),
        compiler_params=_params("parallel"),
        name="proj_residual",
    )(a, w, x2, gate)


def _mlp_kernel(x_ref, gain_ref, shift_ref, scale_ref, gate_ref, w1_ref, w2_ref, fgain_ref,
                o_ref, h_ref, acc_ref, *, final_norm):
    j = pl.program_id(1)

    @pl.when(j == 0)
    def _():
        h = _rms_mod(x_ref[...], gain_ref[...], shift_ref[0], scale_ref[0])
        h_ref[...] = h.astype(BF16)
        acc_ref[...] = jnp.zeros_like(acc_ref)

    a = jnp.dot(h_ref[...], w1_ref[...], preferred_element_type=F32)
    a = jnp.maximum(a, 0.0)
    a = (a * a).astype(BF16)
    acc_ref[...] += jnp.dot(a, w2_ref[...], preferred_element_type=F32)

    @pl.when(j == pl.num_programs(1) - 1)
    def _():
        y = x_ref[...] + gate_ref[0] * acc_ref[...]
        if final_norm:
            inv = lax.rsqrt(jnp.mean(y * y, axis=-1, keepdims=True) + NORM_EPS)
            y = y * inv * fgain_ref[...]
        o_ref[...] = y


def _mlp(x2, gain, shift, scale, gate, w1, w2, fgain, seq, final_norm):
    t, d = x2.shape
    f = w1.shape[1]
    tm = _tile(seq, 512)
    tf = _tile(f, 1024)
    per_b = seq // tm
    return pl.pallas_call(
        functools.partial(_mlp_kernel, final_norm=final_norm),
        grid=(t // tm, f // tf),
        in_specs=[
            pl.BlockSpec((tm, d), lambda i, j: (i, 0)),
            pl.BlockSpec((1, d), lambda i, j: (0, 0)),
            pl.BlockSpec((1, 1, d), lambda i, j: (i // per_b, 0, 0)),
            pl.BlockSpec((1, 1, d), lambda i, j: (i // per_b, 0, 0)),
            pl.BlockSpec((1, 1, d), lambda i, j: (i // per_b, 0, 0)),
            pl.BlockSpec((d, tf), lambda i, j: (0, j)),
            pl.BlockSpec((tf, d), lambda i, j: (j, 0)),
            pl.BlockSpec((1, d), lambda i, j: (0, 0)),
        ],
        out_specs=pl.BlockSpec((tm, d), lambda i, j: (i, 0)),
        out_shape=jax.ShapeDtypeStruct((t, d), F32),
        scratch_shapes=[pltpu.VMEM((tm, d), BF16), pltpu.VMEM((tm, d), F32)],
        compiler_params=_params("parallel", "arbitrary"),
        name="mlp",
    )(x2, gain.reshape(1, d), shift, scale, gate, w1, w2, fgain.reshape(1, d))


def _silu(x):
    h = 0.5 * x
    return h + h * jnp.tanh(h)


def _hgrn_kernel(q_ref, f_ref, i_ref, g_ref, lbl_ref, og_ref, o_ref, st_ref, *,
                 heads, chunk, n_chunks, layer):
    hd = HEAD_DIM

    @pl.when(pl.program_id(2) == 0)
    def _():
        st_ref[...] = jnp.zeros_like(st_ref)

    logits = lbl_ref[...]
    ex = jnp.exp(logits - jnp.max(logits, axis=0, keepdims=True))
    lb = jnp.sum(ex[:layer + 1], axis=0, keepdims=True) / jnp.sum(ex, axis=0, keepdims=True)
    half_gap = 0.5 * (1.0 - lb)
    f_mid = lb + half_gap
    og = og_ref[...]
    row = lax.broadcasted_iota(jnp.int32, (chunk, heads * hd), 0)
    ti = lax.broadcasted_iota(jnp.int32, (chunk, chunk), 0)
    si = lax.broadcasted_iota(jnp.int32, (chunk, chunk), 1)
    causal = si <= ti

    def body(c, carry):
        r0 = pl.multiple_of(c * chunk, chunk)
        rows = pl.ds(r0, chunk)
        swing = half_gap * jnp.tanh(0.5 * f_ref[rows, :])
        log2_f = jnp.log(f_mid + swing) * LOG2E
        k = half_gap - swing
        cum = log2_f
        d = 1
        while d < chunk:
            cum = cum + jnp.where(row >= d, pltpu.roll(cum, d, 0), 0.0)
            d *= 2
        mid = cum[chunk // 2 - 1:chunk // 2, :]
        tot = cum[chunk - 1:chunk, :]
        qs = _silu(q_ref[rows, :])
        q_in = (qs * jnp.exp2(cum - mid)).astype(BF16)
        k_in = (k * jnp.exp2(mid - cum)).astype(BF16)
        q_dec = (qs * jnp.exp2(cum)).astype(BF16)
        k_st = (k * jnp.exp2(tot - cum)).astype(BF16)
        dec = jnp.exp2(tot)
        v = i_ref[rows, :].astype(BF16)
        gs = _silu(g_ref[rows, :])
        for h in range(heads):
            cs = slice(h * hd, (h + 1) * hd)
            sc = lax.dot_general(q_in[:, cs], k_in[:, cs], (((1,), (1,)), ((), ())),
                                 preferred_element_type=F32)
            sc = jnp.where(causal, sc, 0.0).astype(BF16)
            st = st_ref[h]
            o = jnp.dot(sc, v[:, cs], preferred_element_type=F32)
            o = o + lax.dot_general(q_dec[:, cs], st.astype(BF16), (((1,), (1,)), ((), ())),
                                    preferred_element_type=F32)
            upd = lax.dot_general(v[:, cs], k_st[:, cs], (((0,), (0,)), ((), ())),
                                  preferred_element_type=F32)
            st_ref[h] = dec[:, cs] * st + upd
            inv = lax.rsqrt(jnp.mean(o * o, axis=-1, keepdims=True) + NORM_EPS)
            o_ref[rows, cs] = (o * inv * og[:, cs] * gs[:, cs]).astype(o_ref.dtype)
        return carry

    lax.fori_loop(0, n_chunks, body, 0, unroll=HGRN_UNROLL)


def _hgrn(proj, lb_logits, layer, out_gain, batch, seq):
    t, d4 = proj.shape
    d = d4 // 4
    slots = lb_logits.shape[0]
    hb = HGRN_HEADS_PER_STEP
    w = hb * HEAD_DIM
    groups = d // w
    tc = _tile(seq, 512)
    nt = seq // tc

    def part(p):
        return pl.BlockSpec((tc, w), lambda b, hg, s: (b * nt + s, p * groups + hg))

    vec = pl.BlockSpec((1, w), lambda b, hg, s: (0, hg))
    slot_vec = pl.BlockSpec((slots, w), lambda b, hg, s: (0, hg))
    return pl.pallas_call(
        functools.partial(_hgrn_kernel, heads=hb, chunk=HGRN_CHUNK, n_chunks=tc // HGRN_CHUNK,
                          layer=layer),
        grid=(batch, groups, nt),
        in_specs=[part(0), part(1), part(2), part(3), slot_vec, vec],
        out_specs=pl.BlockSpec((tc, w), lambda b, hg, s: (b * nt + s, hg)),
        out_shape=jax.ShapeDtypeStruct((t, d), BF16),
        scratch_shapes=[pltpu.VMEM((hb, HEAD_DIM, HEAD_DIM), F32)],
        compiler_params=_params("parallel", "parallel", "arbitrary"),
        name="hgrn",
    )(proj, proj, proj, proj, lb_logits.astype(F32), out_gain.reshape(1, d))


def _softplus2(z):
    m = jnp.maximum(z, 0.0)
    return m + jnp.log(1.0 + jnp.exp2(z - m - m)) * LOG2E


def _attn_kernel(q_ref, k_ref, v_ref, sel_ref, o_ref, vt_ref, qt_ref, z0, z1, zk0, zk1, hl0, hl1, w0, w1,
                 acc_ref, *, heads, tq, half, tiles):
    z_ref, zk_ref, hl_ref, w_ref = (z0, z1), (zk0, zk1), (hl0, hl1), (w0, w1)
    hd = HEAD_DIM
    n_half = tq // half
    assert n_half == 2, "the two-slot pipeline pairs the two masked blocks of a query tile"
    step = pl.program_id(3)

    @pl.when((pl.program_id(2) == 0) & (step == 0))
    def _():
        def transpose_block(c, carry):
            rs = pl.ds(pl.multiple_of(c * hd, hd), hd)
            vt_ref[c] = v_ref[rs, :].astype(F32).T.astype(BF16)
            return carry
        lax.fori_loop(0, vt_ref.shape[0], transpose_block, 0)

    t_in = lax.broadcasted_iota(jnp.int32, (half, tq), 1)
    s_in = lax.broadcasted_iota(jnp.int32, (half, tq), 0)
    tile_ids = [tiles * step + tl for tl in range(tiles)]
    last_kb = [n_half * ti + (n_half - 1) for ti in tile_ids]
    tile_cols = [slice(tl * heads * tq, (tl + 1) * heads * tq) for tl in range(tiles)]
    units = [(tl, g) for tl in range(tiles) for g in range(heads)]
    cols = [slice(u * tq, (u + 1) * tq) for u in range(len(units))]

    for g, (tl, h) in enumerate(units):
        qt_ref[:, cols[g]] = q_ref[tl * tq:(tl + 1) * tq, h * hd:(h + 1) * hd].astype(F32).T.astype(BF16)

    def pair_start(tl, n):
        return jnp.maximum(last_kb[tl] - n - 1, 0)

    def scores_pair(n):
        for tl in range(tiles):
            ks = pl.ds(pl.multiple_of(pair_start(tl, n) * half, half), 2 * half)
            zz = jnp.dot(k_ref[ks, :], qt_ref[:, tile_cols[tl]], preferred_element_type=F32)
            z_ref[1][:, tile_cols[tl]] = zz[:half]
            z_ref[0][:, tile_cols[tl]] = zz[half:]

    def split(slot, g, masked_n):
        z = z_ref[slot][:, cols[g]]
        sp = _softplus2(z)
        if masked_n is not None:
            sp = jnp.where((s_in + (n_half - 1 - masked_n) * half) < t_in, sp, 0.0)
        hi, lo = _split_bf16(sp)
        hl_ref[slot][:half, cols[g]] = hi
        hl_ref[slot][half:, cols[g]] = lo
        zk_ref[slot][:, cols[g]] = z

    def suffix(slot, g):
        return jnp.dot(sel_ref[...], hl_ref[slot][:, cols[g]], preferred_element_type=F32)

    def weights(slot, g, cs, run, masked_n):
        w = jnp.exp2(zk_ref[slot][:, cols[g]] + cs + run)
        if masked_n is not None:
            w = jnp.where((s_in + (n_half - 1 - masked_n) * half) < t_in, w, 0.0)
        w_ref[slot][:, cols[g]] = w.astype(BF16)
        return run + cs[0:1, :]

    def values_pair(n):
        for tl in range(tiles):
            first = pair_start(tl, n)
            vt = jnp.concatenate([vt_ref[first], vt_ref[first + 1]], axis=1)
            w = jnp.concatenate([w_ref[1][:, tile_cols[tl]], w_ref[0][:, tile_cols[tl]]], axis=0)
            acc_ref[:, tile_cols[tl]] += jnp.dot(vt, w, preferred_element_type=F32)

    hs = range(len(units))

    def even_iteration(i, runs, masked_split, masked_weights, with_values=True):
        for g in hs:
            split(1, g, masked_split)
        cs = [suffix(0, g) for g in hs]
        if with_values:
            values_pair(i - 2)
        scores_pair(i + 2)
        return [weights(0, g, cs[g], runs[g], masked_weights) for g in hs]

    def odd_iteration(runs, masked_weights):
        for g in hs:
            split(0, g, None)
        cs = [suffix(1, g) for g in hs]
        return [weights(1, g, cs[g], runs[g], masked_weights) for g in hs]

    acc_ref[...] = jnp.zeros_like(acc_ref)
    scores_pair(0)
    for g in hs:
        split(0, g, 0)
    runs = [jnp.zeros((1, tq), F32) for _ in hs]
    runs = even_iteration(0, runs, 1, 0, with_values=False)
    runs = odd_iteration(runs, 1)

    def keep_walking(carry):
        j, live, _ = carry
        return jnp.logical_and(j < tile_ids[-1], live > 0)

    def body(carry):
        j, _, runs = carry
        runs = [jnp.where(j >= tile_ids[tl], ATTN_EXHAUSTED_LOG2, runs[g]) for g, (tl, _) in enumerate(units)]
        runs = even_iteration(2 + 2 * j, runs, None, None)
        runs = odd_iteration(runs, None)
        slowest = functools.reduce(jnp.maximum, runs)
        live = (jnp.max(slowest) > ATTN_DEAD_LOG2).astype(jnp.int32)
        return j + 1, live, tuple(runs)

    pairs_done, _, _ = lax.while_loop(keep_walking, body, (jnp.int32(0), jnp.int32(1), tuple(runs)))
    values_pair(n_half * pairs_done)
    for g, (tl, h) in enumerate(units):
        o_ref[tl * tq:(tl + 1) * tq, h * hd:(h + 1) * hd] = acc_ref[:, cols[g]].T.astype(o_ref.dtype)


def _attention(q, kv, batch, seq):
    t, d = q.shape
    hd = HEAD_DIM
    kvh = kv.shape[1] // (2 * hd)
    group = d // hd // kvh
    heads = _tile(group, ATTN_HEADS)
    tq = _tile(seq, ATTN_Q)
    half = _tile(tq, ATTN_HALF)
    tiles = _tile(seq // tq, ATTN_TILES)
    steps = seq // (tq * tiles)
    gsteps = group // heads
    lanes = tiles * heads * tq
    si = lax.broadcasted_iota(jnp.int32, (half, half), 0)
    ji = lax.broadcasted_iota(jnp.int32, (half, half), 1)
    neg_suffix = -(ji >= si).astype(BF16)
    sel = jnp.concatenate([neg_suffix, neg_suffix], axis=1)
    return pl.pallas_call(
        functools.partial(_attn_kernel, heads=heads, tq=tq, half=half, tiles=tiles),
        grid=(batch, kvh, gsteps, steps),
        in_specs=[
            pl.BlockSpec((tiles * tq, heads * hd), lambda b, h, g, i: (b * steps + i, h * gsteps + g)),
            pl.BlockSpec((seq, hd), lambda b, h, g, i: (b, h)),
            pl.BlockSpec((seq, hd), lambda b, h, g, i: (b, kvh + h)),
            pl.BlockSpec((half, 2 * half), lambda b, h, g, i: (0, 0)),
        ],
        out_specs=pl.BlockSpec((tiles * tq, heads * hd), lambda b, h, g, i: (b * steps + i, h * gsteps + g)),
        out_shape=jax.ShapeDtypeStruct((t, d), BF16),
        scratch_shapes=[
            pltpu.VMEM((seq // hd, hd, hd), BF16),
            pltpu.VMEM((hd, lanes), BF16),
            pltpu.VMEM((half, lanes), F32),
            pltpu.VMEM((half, lanes), F32),
            pltpu.VMEM((half, lanes), F32),
            pltpu.VMEM((half, lanes), F32),
            pltpu.VMEM((2 * half, lanes), BF16),
            pltpu.VMEM((2 * half, lanes), BF16),
            pltpu.VMEM((half, lanes), BF16),
            pltpu.VMEM((half, lanes), BF16),
            pltpu.VMEM((hd, lanes), F32),
        ],
        compiler_params=_params("parallel", "parallel", "arbitrary", "arbitrary"),
        name="stick_breaking",
    )(q, kv, kv, sel)


def kernel(x, c, ada_w, ada_b, norm_mix, norm_mlp, a_w_in, a_lb_logits, a_out_gain, a_w_out,
           kv_ada_w, kv_ada_b, kv_norm, w_kv, b_w_q, b_w_out, mlp_w1, mlp_w2, final_norm):
    b, s, d = x.shape
    depth = ada_w.shape[0]
    n_a = a_w_in.shape[0]
    x2 = x.reshape(b * s, d)

    mod = _conditioning(c, ada_w, ada_b).reshape(depth, b, 6, 1, d)
    kv_mod = _conditioning(c, kv_ada_w[None], kv_ada_b[None]).reshape(b, 2, 1, d)

    kv = None
    for layer in range(depth):
        sh1, sc1, g1, sh2, sc2, g2 = (mod[layer, :, p] for p in range(6))
        if layer < n_a:
            proj = _norm_proj(x2, norm_mix[layer], sh1, sc1, a_w_in[layer].astype(BF16), s, F32)
            o = _hgrn(proj, a_lb_logits, layer, a_out_gain[layer], b, s)
            w_out = a_w_out[layer]
        else:
            j = layer - n_a
            q = _norm_proj(x2, norm_mix[layer], sh1, sc1, b_w_q[j].astype(BF16), s, BF16,
                           out_scale=LOG2E * HEAD_DIM ** -0.5)
            o = _attention(q, kv, b, s)
            w_out = b_w_out[j]
        x2 = _proj_residual(o, w_out.astype(BF16), x2, g1, s)
        last = layer == depth - 1
        x2 = _mlp(x2, norm_mlp[layer], sh2, sc2, g2, mlp_w1[layer].astype(BF16),
                  mlp_w2[layer].astype(BF16), final_norm, s, final_norm=last)
        if layer == n_a - 1:
            kv = _norm_proj(x2, kv_norm, kv_mod[:, 0], kv_mod[:, 1], w_kv.astype(BF16), s, BF16)
    return x2.reshape(b, s, d)
```

```python
import functools

import jax
import jax.numpy as jnp
from jax import lax
from jax.experimental import pallas as pl
from jax.experimental.pallas import tpu as pltpu

HEAD_DIM = 128
NORM_EPS = 1e-6
HGRN_CHUNK = 32
HGRN_HEADS_PER_STEP = 4
HGRN_UNROLL = 16
ATTN_Q = 256
ATTN_HEADS = 4
ATTN_HALF = 128
ATTN_TILES = 2
ATTN_DEAD_LOG2 = -160.0
ATTN_EXHAUSTED_LOG2 = -1e30
LOG2E = 1.4426950408889634
VMEM_LIMIT_BYTES = 58 * 1024 * 1024
CAST_BLOCK_ELEMS = 2 * 1024 * 1024

F32 = jnp.float32
BF16 = jnp.bfloat16


def _tile(n, pref):
    t = min(n, pref)
    while n % t:
        t -= 1
    return t


def _params(*sem):
    return pltpu.CompilerParams(dimension_semantics=sem, vmem_limit_bytes=VMEM_LIMIT_BYTES)


def _split_bf16(a):
    hi = a.astype(BF16)
    lo = (a - hi.astype(F32)).astype(BF16)
    return hi, lo


def _rms_mod(x, gain, shift, scale):
    inv = lax.rsqrt(jnp.mean(x * x, axis=-1, keepdims=True) + NORM_EPS)
    return (x * inv) * (gain * (1.0 + scale)) + shift


def _cast_kernel(w_ref, o_ref):
    o_ref[...] = w_ref[...].astype(BF16)


def _weight_bf16(w, layer=0):
    if w.ndim == 2:
        w = w[None]
    _, r, c = w.shape
    rb = _tile(r, max(8, CAST_BLOCK_ELEMS // c))
    return pl.pallas_call(
        _cast_kernel,
        grid=(r // rb,),
        in_specs=[pl.BlockSpec((None, rb, c), lambda i: (layer, i, 0))],
        out_specs=pl.BlockSpec((rb, c), lambda i: (i, 0)),
        out_shape=jax.ShapeDtypeStruct((r, c), BF16),
        compiler_params=_params("parallel"),
        name="weight_bf16",
    )(w)


def _cond_kernel(c_ref, w_ref, b_ref, o_ref):
    c = c_ref[...]
    c_act = c * (1.0 / (1.0 + jnp.exp(-c)))
    c_hi, c_lo = _split_bf16(c_act)
    w_hi, w_lo = _split_bf16(w_ref[...])
    lhs = jnp.concatenate([c_hi, c_lo], axis=0)
    acc = jnp.dot(lhs, w_hi, preferred_element_type=F32)
    nb = c.shape[0]
    out = acc[:nb] + acc[nb:] + jnp.dot(c_hi, w_lo, preferred_element_type=F32)
    o_ref[...] = out + b_ref[...]


def _conditioning(c, w, b):
    nl, d, n = w.shape
    nb = c.shape[0]
    tn = _tile(n, 1024)
    return pl.pallas_call(
        _cond_kernel,
        grid=(nl, n // tn),
        in_specs=[
            pl.BlockSpec((nb, d), lambda l, j: (0, 0)),
            pl.BlockSpec((None, d, tn), lambda l, j: (l, 0, j)),
            pl.BlockSpec((None, 1, tn), lambda l, j: (l, 0, j)),
        ],
        out_specs=pl.BlockSpec((None, nb, tn), lambda l, j: (l, 0, j)),
        out_shape=jax.ShapeDtypeStruct((nl, nb, n), F32),
        compiler_params=_params("parallel", "parallel"),
        name="conditioning",
    )(c, w, b.reshape(nl, 1, n))


def _norm_proj_kernel(x_ref, gain_ref, shift_ref, scale_ref, w_ref, o_ref, h_ref, *, out_scale):
    @pl.when(pl.program_id(1) == 0)
    def _():
        h = _rms_mod(x_ref[...], gain_ref[...], shift_ref[0], scale_ref[0])
        h_ref[...] = h.astype(BF16)

    acc = jnp.dot(h_ref[...], w_ref[...], preferred_element_type=F32)
    if out_scale != 1.0:
        acc = acc * out_scale
    o_ref[...] = acc.astype(o_ref.dtype)


def _norm_proj(x2, gain, shift, scale, w, seq, out_dtype, out_scale=1.0):
    t, d = x2.shape
    n = w.shape[1]
    tm = _tile(seq, 1024)
    tn = _tile(n, 2048)
    per_b = seq // tm
    return pl.pallas_call(
        functools.partial(_norm_proj_kernel, out_scale=out_scale),
        grid=(t // tm, n // tn),
        in_specs=[
            pl.BlockSpec((tm, d), lambda i, j: (i, 0)),
            pl.BlockSpec((1, d), lambda i, j: (0, 0)),
            pl.BlockSpec((1, 1, d), lambda i, j: (i // per_b, 0, 0)),
            pl.BlockSpec((1, 1, d), lambda i, j: (i // per_b, 0, 0)),
            pl.BlockSpec((d, tn), lambda i, j: (0, j)),
        ],
        out_specs=pl.BlockSpec((tm, tn), lambda i, j: (i, j)),
        out_shape=jax.ShapeDtypeStruct((t, n), out_dtype),
        scratch_shapes=[pltpu.VMEM((tm, d), BF16)],
        compiler_params=_params("parallel", "arbitrary"),
        name="norm_proj",
    )(x2, gain.reshape(1, d), shift, scale, w)


def _proj_residual_kernel(a_ref, w_ref, x_ref, gate_ref, o_ref):
    y = jnp.dot(a_ref[...], w_ref[...], preferred_element_type=F32)
    o_ref[...] = x_ref[...] + gate_ref[0] * y


def _proj_residual(a, w, x2, gate, seq):
    t, k = a.shape
    d = w.shape[1]
    tm = _tile(seq, 512)
    per_b = seq // tm
    return pl.pallas_call(
        _proj_residual_kernel,
        grid=(t // tm,),
        in_specs=[
            pl.BlockSpec((tm, k), lambda i: (i, 0)),
            pl.BlockSpec((k, d), lambda i: (0, 0)),
            pl.BlockSpec((tm, d), lambda i: (i, 0)),
            pl.BlockSpec((1, 1, d), lambda i: (i // per_b, 0, 0)),
        ],
        out_specs=pl.BlockSpec((tm, d), lambda i: (i, 0)),
        out_shape=jax.ShapeDtypeStruct((t, d), F32),
        compiler_params=_params("parallel"),
        name="proj_residual",
    )(a, w, x2, gate)


def _mlp_kernel(x_ref, gain_ref, shift_ref, scale_ref, gate_ref, w1_ref, w2_ref, fgain_ref,
                o_ref, h_ref, acc_ref, *, final_norm):
    j = pl.program_id(1)

    @pl.when(j == 0)
    def _():
        h = _rms_mod(x_ref[...], gain_ref[...], shift_ref[0], scale_ref[0])
        h_ref[...] = h.astype(BF16)
        acc_ref[...] = jnp.zeros_like(acc_ref)

    a = jnp.dot(h_ref[...], w1_ref[...], preferred_element_type=F32)
    a = jnp.maximum(a, 0.0)
    a = (a * a).astype(BF16)
    acc_ref[...] += jnp.dot(a, w2_ref[...], preferred_element_type=F32)

    @pl.when(j == pl.num_programs(1) - 1)
    def _():
        y = x_ref[...] + gate_ref[0] * acc_ref[...]
        if final_norm:
            inv = lax.rsqrt(jnp.mean(y * y, axis=-1, keepdims=True) + NORM_EPS)
            y = y * inv * fgain_ref[...]
        o_ref[...] = y


def _mlp(x2, gain, shift, scale, gate, w1, w2, fgain, seq, final_norm):
    t, d = x2.shape
    f = w1.shape[1]
    tm = _tile(seq, 512)
    tf = _tile(f, 1024)
    per_b = seq // tm
    return pl.pallas_call(
        functools.partial(_mlp_kernel, final_norm=final_norm),
        grid=(t // tm, f // tf),
        in_specs=[
            pl.BlockSpec((tm, d), lambda i, j: (i, 0)),
            pl.BlockSpec((1, d), lambda i, j: (0, 0)),
            pl.BlockSpec((1, 1, d), lambda i, j: (i // per_b, 0, 0)),
            pl.BlockSpec((1, 1, d), lambda i, j: (i // per_b, 0, 0)),
            pl.BlockSpec((1, 1, d), lambda i, j: (i // per_b, 0, 0)),
            pl.BlockSpec((d, tf), lambda i, j: (0, j)),
            pl.BlockSpec((tf, d), lambda i, j: (j, 0)),
            pl.BlockSpec((1, d), lambda i, j: (0, 0)),
        ],
        out_specs=pl.BlockSpec((tm, d), lambda i, j: (i, 0)),
        out_shape=jax.ShapeDtypeStruct((t, d), F32),
        scratch_shapes=[pltpu.VMEM((tm, d), BF16), pltpu.VMEM((tm, d), F32)],
        compiler_params=_params("parallel", "arbitrary"),
        name="mlp",
    )(x2, gain.reshape(1, d), shift, scale, gate, w1, w2, fgain.reshape(1, d))


def _silu(x):
    h = 0.5 * x
    return h + h * jnp.tanh(h)


def _hgrn_kernel(q_ref, f_ref, i_ref, g_ref, lbl_ref, og_ref, o_ref, st_ref, *,
                 heads, chunk, n_chunks, layer):
    hd = HEAD_DIM

    @pl.when(pl.program_id(2) == 0)
    def _():
        st_ref[...] = jnp.zeros_like(st_ref)

    logits = lbl_ref[...]
    ex = jnp.exp(logits - jnp.max(logits, axis=0, keepdims=True))
    lb = jnp.sum(ex[:layer + 1], axis=0, keepdims=True) / jnp.sum(ex, axis=0, keepdims=True)
    half_gap = 0.5 * (1.0 - lb)
    f_mid = lb + half_gap
    og = og_ref[...]
    row = lax.broadcasted_iota(jnp.int32, (chunk, heads * hd), 0)
    ti = lax.broadcasted_iota(jnp.int32, (chunk, chunk), 0)
    si = lax.broadcasted_iota(jnp.int32, (chunk, chunk), 1)
    causal = si <= ti

    def body(c, carry):
        r0 = pl.multiple_of(c * chunk, chunk)
        rows = pl.ds(r0, chunk)
        swing = half_gap * jnp.tanh(0.5 * f_ref[rows, :])
        log2_f = jnp.log(f_mid + swing) * LOG2E
        k = half_gap - swing
        cum = log2_f
        d = 1
        while d < chunk:
            cum = cum + jnp.where(row >= d, pltpu.roll(cum, d, 0), 0.0)
            d *= 2
        mid = cum[chunk // 2 - 1:chunk // 2, :]
        tot = cum[chunk - 1:chunk, :]
        qs = _silu(q_ref[rows, :])
        q_in = (qs * jnp.exp2(cum - mid)).astype(BF16)
        k_in = (k * jnp.exp2(mid - cum)).astype(BF16)
        q_dec = (qs * jnp.exp2(cum)).astype(BF16)
        k_st = (k * jnp.exp2(tot - cum)).astype(BF16)
        dec = jnp.exp2(tot)
        v = i_ref[rows, :].astype(BF16)
        gs = _silu(g_ref[rows, :])
        for h in range(heads):
            cs = slice(h * hd, (h + 1) * hd)
            sc = lax.dot_general(q_in[:, cs], k_in[:, cs], (((1,), (1,)), ((), ())),
                                 preferred_element_type=F32)
            sc = jnp.where(causal, sc, 0.0).astype(BF16)
            st = st_ref[h]
            o = jnp.dot(sc, v[:, cs], preferred_element_type=F32)
            o = o + lax.dot_general(q_dec[:, cs], st.astype(BF16), (((1,), (1,)), ((), ())),
                                    preferred_element_type=F32)
            upd = lax.dot_general(v[:, cs], k_st[:, cs], (((0,), (0,)), ((), ())),
                                  preferred_element_type=F32)
            st_ref[h] = dec[:, cs] * st + upd
            inv = lax.rsqrt(jnp.mean(o * o, axis=-1, keepdims=True) + NORM_EPS)
            o_ref[rows, cs] = (o * inv * og[:, cs] * gs[:, cs]).astype(o_ref.dtype)
        return carry

    lax.fori_loop(0, n_chunks, body, 0, unroll=HGRN_UNROLL)


def _hgrn(proj, lb_logits, layer, out_gain, batch, seq):
    t, d4 = proj.shape
    d = d4 // 4
    slots = lb_logits.shape[0]
    hb = HGRN_HEADS_PER_STEP
    w = hb * HEAD_DIM
    groups = d // w
    tc = _tile(seq, 512)
    nt = seq // tc

    def part(p):
        return pl.BlockSpec((tc, w), lambda b, hg, s: (b * nt + s, p * groups + hg))

    vec = pl.BlockSpec((1, w), lambda b, hg, s: (0, hg))
    slot_vec = pl.BlockSpec((slots, w), lambda b, hg, s: (0, hg))
    return pl.pallas_call(
        functools.partial(_hgrn_kernel, heads=hb, chunk=HGRN_CHUNK, n_chunks=tc // HGRN_CHUNK,
                          layer=layer),
        grid=(batch, groups, nt),
        in_specs=[part(0), part(1), part(2), part(3), slot_vec, vec],
        out_specs=pl.BlockSpec((tc, w), lambda b, hg, s: (b * nt + s, hg)),
        out_shape=jax.ShapeDtypeStruct((t, d), BF16),
        scratch_shapes=[pltpu.VMEM((hb, HEAD_DIM, HEAD_DIM), F32)],
        compiler_params=_params("parallel", "parallel", "arbitrary"),
        name="hgrn",
    )(proj, proj, proj, proj, lb_logits.astype(F32), out_gain.reshape(1, d))


def _softplus2(z):
    m = jnp.maximum(z, 0.0)
    return m + jnp.log(1.0 + jnp.exp2(z - m - m)) * LOG2E


def _attn_kernel(q_ref, k_ref, v_ref, sel_ref, o_ref, vt_ref, qt_ref, z0, z1, zk0, zk1, hl0, hl1, w0, w1,
                 acc_ref, *, heads, tq, half, tiles):
    z_ref, zk_ref, hl_ref, w_ref = (z0, z1), (zk0, zk1), (hl0, hl1), (w0, w1)
    hd = HEAD_DIM
    n_half = tq // half
    assert n_half == 2, "the two-slot pipeline pairs the two masked blocks of a query tile"
    step = pl.program_id(3)

    @pl.when((pl.program_id(2) == 0) & (step == 0))
    def _():
        def transpose_block(c, carry):
            rs = pl.ds(pl.multiple_of(c * hd, hd), hd)
            vt_ref[c] = v_ref[rs, :].astype(F32).T.astype(BF16)
            return carry
        lax.fori_loop(0, vt_ref.shape[0], transpose_block, 0, unroll=8)

    t_in = lax.broadcasted_iota(jnp.int32, (half, tq), 1)
    s_in = lax.broadcasted_iota(jnp.int32, (half, tq), 0)
    tile_ids = [tiles * step + tl for tl in range(tiles)]
    last_kb = [n_half * ti + (n_half - 1) for ti in tile_ids]
    units = [(tl, g) for tl in range(tiles) for g in range(heads)]
    tile_units = [range(tl * heads, (tl + 1) * heads) for tl in range(tiles)]

    def tile_wide(ref, tl):
        return jnp.concatenate([ref[u] for u in tile_units[tl]], axis=1)

    for g, (tl, h) in enumerate(units):
        qt_ref[g] = q_ref[tl * tq:(tl + 1) * tq, h * hd:(h + 1) * hd].astype(F32).T.astype(BF16)

    def pair_start(tl, n):
        return jnp.maximum(last_kb[tl] - n - 1, 0)

    def scores_pair(n):
        for tl in range(tiles):
            ks = pl.ds(pl.multiple_of(pair_start(tl, n) * half, half), 2 * half)
            zz = jnp.dot(k_ref[ks, :], tile_wide(qt_ref, tl), preferred_element_type=F32)
            for g, u in enumerate(tile_units[tl]):
                z_ref[1][u] = zz[:half, g * tq:(g + 1) * tq]
                z_ref[0][u] = zz[half:, g * tq:(g + 1) * tq]

    def split(slot, g, masked_n):
        z = z_ref[slot][g]
        sp = _softplus2(z)
        if masked_n is not None:
            sp = jnp.where((s_in + (n_half - 1 - masked_n) * half) < t_in, sp, 0.0)
        hi, lo = _split_bf16(sp)
        hl_ref[slot][g, :half, :] = hi
        hl_ref[slot][g, half:, :] = lo
        zk_ref[slot][g] = z

    def suffix(slot, g):
        return jnp.dot(sel_ref[...], hl_ref[slot][g], preferred_element_type=F32)

    def weights(slot, g, cs, run, masked_n):
        w = jnp.exp2(zk_ref[slot][g] + cs + run)
        if masked_n is not None:
            w = jnp.where((s_in + (n_half - 1 - masked_n) * half) < t_in, w, 0.0)
        w_ref[slot][g] = w.astype(BF16)
        return run + cs[0:1, :]

    def values_pair(n):
        for tl in range(tiles):
            first = pair_start(tl, n)
            vt = jnp.concatenate([vt_ref[first], vt_ref[first + 1]], axis=1)
            w = jnp.concatenate([tile_wide(w_ref[1], tl), tile_wide(w_ref[0], tl)], axis=0)
            out = jnp.dot(vt, w, preferred_element_type=F32)
            for g, u in enumerate(tile_units[tl]):
                acc_ref[u] += out[:, g * tq:(g + 1) * tq]

    hs = range(len(units))

    def even_iteration(i, runs, masked_split, masked_weights, with_values=True):
        for g in hs:
            split(1, g, masked_split)
        cs = [suffix(0, g) for g in hs]
        if with_values:
            values_pair(i - 2)
        scores_pair(i + 2)
        return [weights(0, g, cs[g], runs[g], masked_weights) for g in hs]

    def odd_iteration(runs, masked_weights):
        for g in hs:
            split(0, g, None)
        cs = [suffix(1, g) for g in hs]
        return [weights(1, g, cs[g], runs[g], masked_weights) for g in hs]

    acc_ref[...] = jnp.zeros_like(acc_ref)
    scores_pair(0)
    for g in hs:
        split(0, g, 0)
    runs = [jnp.zeros((1, tq), F32) for _ in hs]
    runs = even_iteration(0, runs, 1, 0, with_values=False)
    runs = odd_iteration(runs, 1)

    def keep_walking(carry):
        j, live, _ = carry
        return jnp.logical_and(j < tile_ids[-1], live > 0)

    def body(carry):
        j, _, runs = carry
        runs = [jnp.where(j >= tile_ids[tl], ATTN_EXHAUSTED_LOG2, runs[g]) for g, (tl, _) in enumerate(units)]
        runs = even_iteration(2 + 2 * j, runs, None, None)
        runs = odd_iteration(runs, None)
        slowest = functools.reduce(jnp.maximum, runs)
        live = (jnp.max(slowest) > ATTN_DEAD_LOG2).astype(jnp.int32)
        return j + 1, live, tuple(runs)

    pairs_done, _, _ = lax.while_loop(keep_walking, body, (jnp.int32(0), jnp.int32(1), tuple(runs)))
    values_pair(n_half * pairs_done)
    for g, (tl, h) in enumerate(units):
        o_ref[tl * tq:(tl + 1) * tq, h * hd:(h + 1) * hd] = acc_ref[g].T.astype(o_ref.dtype)


def _attention(q, kv, batch, seq):
    t, d = q.shape
    hd = HEAD_DIM
    kvh = kv.shape[1] // (2 * hd)
    group = d // hd // kvh
    heads = _tile(group, ATTN_HEADS)
    tq = _tile(seq, ATTN_Q)
    half = _tile(tq, ATTN_HALF)
    tiles = _tile(seq // tq, ATTN_TILES)
    steps = seq // (tq * tiles)
    gsteps = group // heads
    n_units = tiles * heads
    si = lax.broadcasted_iota(jnp.int32, (half, half), 0)
    ji = lax.broadcasted_iota(jnp.int32, (half, half), 1)
    neg_suffix = -(ji >= si).astype(BF16)
    sel = jnp.concatenate([neg_suffix, neg_suffix], axis=1)
    return pl.pallas_call(
        functools.partial(_attn_kernel, heads=heads, tq=tq, half=half, tiles=tiles),
        grid=(batch, kvh, gsteps, steps),
        in_specs=[
            pl.BlockSpec((tiles * tq, heads * hd), lambda b, h, g, i: (b * steps + i, h * gsteps + g)),
            pl.BlockSpec((seq, hd), lambda b, h, g, i: (b, h)),
            pl.BlockSpec((seq, hd), lambda b, h, g, i: (b, kvh + h)),
            pl.BlockSpec((half, 2 * half), lambda b, h, g, i: (0, 0)),
        ],
        out_specs=pl.BlockSpec((tiles * tq, heads * hd), lambda b, h, g, i: (b * steps + i, h * gsteps + g)),
        out_shape=jax.ShapeDtypeStruct((t, d), BF16),
        scratch_shapes=[
            pltpu.VMEM((seq // hd, hd, hd), BF16),
            pltpu.VMEM((n_units, hd, tq), BF16),
            pltpu.VMEM((n_units, half, tq), F32),
            pltpu.VMEM((n_units, half, tq), F32),
            pltpu.VMEM((n_units, half, tq), F32),
            pltpu.VMEM((n_units, half, tq), F32),
            pltpu.VMEM((n_units, 2 * half, tq), BF16),
            pltpu.VMEM((n_units, 2 * half, tq), BF16),
            pltpu.VMEM((n_units, half, tq), BF16),
            pltpu.VMEM((n_units, half, tq), BF16),
            pltpu.VMEM((n_units, hd, tq), F32),
        ],
        compiler_params=_params("parallel", "parallel", "arbitrary", "arbitrary"),
        name="stick_breaking",
    )(q, kv, kv, sel)


def kernel(x, c, ada_w, ada_b, norm_mix, norm_mlp, a_w_in, a_lb_logits, a_out_gain, a_w_out,
           kv_ada_w, kv_ada_b, kv_norm, w_kv, b_w_q, b_w_out, mlp_w1, mlp_w2, final_norm):
    b, s, d = x.shape
    depth = ada_w.shape[0]
    n_a = a_w_in.shape[0]
    x2 = x.reshape(b * s, d)

    mod = _conditioning(c, ada_w, ada_b).reshape(depth, b, 6, 1, d)
    kv_mod = _conditioning(c, kv_ada_w[None], kv_ada_b[None]).reshape(b, 2, 1, d)

    kv = None
    for layer in range(depth):
        sh1, sc1, g1, sh2, sc2, g2 = (mod[layer, :, p] for p in range(6))
        if layer < n_a:
            proj = _norm_proj(x2, norm_mix[layer], sh1, sc1, _weight_bf16(a_w_in, layer), s, F32)
            o = _hgrn(proj, a_lb_logits, layer, a_out_gain[layer], b, s)
            w_out = _weight_bf16(a_w_out, layer)
        else:
            j = layer - n_a
            q = _norm_proj(x2, norm_mix[layer], sh1, sc1, _weight_bf16(b_w_q, j), s, BF16,
                           out_scale=LOG2E * HEAD_DIM ** -0.5)
            o = _attention(q, kv, b, s)
            w_out = _weight_bf16(b_w_out, j)
        x2 = _proj_residual(o, w_out, x2, g1, s)
        last = layer == depth - 1
        x2 = _mlp(x2, norm_mlp[layer], sh2, sc2, g2, _weight_bf16(mlp_w1, layer),
                  _weight_bf16(mlp_w2, layer), final_norm, s, final_norm=last)
        if layer == n_a - 1:
            kv = _norm_proj(x2, kv_norm, kv_mod[:, 0], kv_mod[:, 1], _weight_bf16(w_kv), s, BF16)
    return x2.reshape(b, s, d)
```

```python
import functools

import jax
import jax.numpy as jnp
from jax import lax
from jax.experimental import pallas as pl
from jax.experimental.pallas import tpu as pltpu

HEAD_DIM = 128
NORM_EPS = 1e-6
HGRN_CHUNK = 32
HGRN_HEADS_PER_STEP = 8
HGRN_UNROLL = 16
ATTN_Q = 256
ATTN_HEADS = 4
ATTN_HALF = 128
ATTN_TILES = 4
ATTN_DEAD_LOG2 = -160.0
ATTN_EXHAUSTED_LOG2 = -1e30
LOG2E = 1.4426950408889634
VMEM_LIMIT_BYTES = 58 * 1024 * 1024
CAST_BLOCK_ELEMS = 2 * 1024 * 1024

F32 = jnp.float32
BF16 = jnp.bfloat16


def _tile(n, pref):
    t = min(n, pref)
    while n % t:
        t -= 1
    return t


def _params(*sem):
    return pltpu.CompilerParams(dimension_semantics=sem, vmem_limit_bytes=VMEM_LIMIT_BYTES)


def _split_bf16(a):
    hi = a.astype(BF16)
    lo = (a - hi.astype(F32)).astype(BF16)
    return hi, lo


def _rms_mod(x, gain, shift, scale):
    inv = lax.rsqrt(jnp.mean(x * x, axis=-1, keepdims=True) + NORM_EPS)
    return (x * inv) * (gain * (1.0 + scale)) + shift


def _cast_kernel(w_ref, o_ref):
    o_ref[...] = w_ref[...].astype(BF16)


def _weight_bf16(w, layer=0):
    if w.ndim == 2:
        w = w[None]
    _, r, c = w.shape
    rb = _tile(r, max(8, CAST_BLOCK_ELEMS // c))
    return pl.pallas_call(
        _cast_kernel,
        grid=(r // rb,),
        in_specs=[pl.BlockSpec((None, rb, c), lambda i: (layer, i, 0))],
        out_specs=pl.BlockSpec((rb, c), lambda i: (i, 0)),
        out_shape=jax.ShapeDtypeStruct((r, c), BF16),
        compiler_params=_params("parallel"),
        name="weight_bf16",
    )(w)


def _cond_kernel(c_ref, w_ref, b_ref, o_ref):
    c = c_ref[...]
    c_act = c * (1.0 / (1.0 + jnp.exp(-c)))
    c_hi, c_lo = _split_bf16(c_act)
    w_hi, w_lo = _split_bf16(w_ref[...])
    lhs = jnp.concatenate([c_hi, c_lo], axis=0)
    acc = jnp.dot(lhs, w_hi, preferred_element_type=F32)
    nb = c.shape[0]
    out = acc[:nb] + acc[nb:] + jnp.dot(c_hi, w_lo, preferred_element_type=F32)
    o_ref[...] = out + b_ref[...]


def _conditioning(c, w, b):
    nl, d, n = w.shape
    nb = c.shape[0]
    tn = _tile(n, 1024)
    return pl.pallas_call(
        _cond_kernel,
        grid=(nl, n // tn),
        in_specs=[
            pl.BlockSpec((nb, d), lambda l, j: (0, 0)),
            pl.BlockSpec((None, d, tn), lambda l, j: (l, 0, j)),
            pl.BlockSpec((None, 1, tn), lambda l, j: (l, 0, j)),
        ],
        out_specs=pl.BlockSpec((None, nb, tn), lambda l, j: (l, 0, j)),
        out_shape=jax.ShapeDtypeStruct((nl, nb, n), F32),
        compiler_params=_params("parallel", "parallel"),
        name="conditioning",
    )(c, w, b.reshape(nl, 1, n))


def _norm_proj_kernel(x_ref, gain_ref, shift_ref, scale_ref, w_ref, o_ref, h_ref, *, out_scale):
    @pl.when(pl.program_id(1) == 0)
    def _():
        h = _rms_mod(x_ref[...], gain_ref[...], shift_ref[0], scale_ref[0])
        h_ref[...] = h.astype(BF16)

    acc = jnp.dot(h_ref[...], w_ref[...], preferred_element_type=F32)
    if out_scale != 1.0:
        acc = acc * out_scale
    o_ref[...] = acc.astype(o_ref.dtype)


def _norm_proj(x2, gain, shift, scale, w, seq, out_dtype, out_scale=1.0):
    t, d = x2.shape
    n = w.shape[1]
    tm = _tile(seq, 1024)
    tn = _tile(n, 2048)
    per_b = seq // tm
    return pl.pallas_call(
        functools.partial(_norm_proj_kernel, out_scale=out_scale),
        grid=(t // tm, n // tn),
        in_specs=[
            pl.BlockSpec((tm, d), lambda i, j: (i, 0)),
            pl.BlockSpec((1, d), lambda i, j: (0, 0)),
            pl.BlockSpec((1, 1, d), lambda i, j: (i // per_b, 0, 0)),
            pl.BlockSpec((1, 1, d), lambda i, j: (i // per_b, 0, 0)),
            pl.BlockSpec((d, tn), lambda i, j: (0, j)),
        ],
        out_specs=pl.BlockSpec((tm, tn), lambda i, j: (i, j)),
        out_shape=jax.ShapeDtypeStruct((t, n), out_dtype),
        scratch_shapes=[pltpu.VMEM((tm, d), BF16)],
        compiler_params=_params("parallel", "arbitrary"),
        name="norm_proj",
    )(x2, gain.reshape(1, d), shift, scale, w)


def _proj_residual_kernel(a_ref, w_ref, x_ref, gate_ref, o_ref):
    y = jnp.dot(a_ref[...], w_ref[...], preferred_element_type=F32)
    o_ref[...] = x_ref[...] + gate_ref[0] * y


def _proj_residual(a, w, x2, gate, seq):
    t, k = a.shape
    d = w.shape[1]
    tm = _tile(seq, 512)
    per_b = seq // tm
    return pl.pallas_call(
        _proj_residual_kernel,
        grid=(t // tm,),
        in_specs=[
            pl.BlockSpec((tm, k), lambda i: (i, 0)),
            pl.BlockSpec((k, d), lambda i: (0, 0)),
            pl.BlockSpec((tm, d), lambda i: (i, 0)),
            pl.BlockSpec((1, 1, d), lambda i: (i // per_b, 0, 0)),
        ],
        out_specs=pl.BlockSpec((tm, d), lambda i: (i, 0)),
        out_shape=jax.ShapeDtypeStruct((t, d), F32),
        compiler_params=_params("parallel"),
        name="proj_residual",
    )(a, w, x2, gate)


def _mlp_kernel(x_ref, gain_ref, shift_ref, scale_ref, gate_ref, w1_ref, w2_ref, fgain_ref,
                o_ref, h_ref, acc_ref, *, final_norm):
    j = pl.program_id(1)

    @pl.when(j == 0)
    def _():
        h = _rms_mod(x_ref[...], gain_ref[...], shift_ref[0], scale_ref[0])
        h_ref[...] = h.astype(BF16)
        acc_ref[...] = jnp.zeros_like(acc_ref)

    a = jnp.dot(h_ref[...], w1_ref[...], preferred_element_type=F32)
    a = jnp.maximum(a, 0.0)
    a = (a * a).astype(BF16)
    acc_ref[...] += jnp.dot(a, w2_ref[...], preferred_element_type=F32)

    @pl.when(j == pl.num_programs(1) - 1)
    def _():
        y = x_ref[...] + gate_ref[0] * acc_ref[...]
        if final_norm:
            inv = lax.rsqrt(jnp.mean(y * y, axis=-1, keepdims=True) + NORM_EPS)
            y = y * inv * fgain_ref[...]
        o_ref[...] = y


def _mlp(x2, gain, shift, scale, gate, w1, w2, fgain, seq, final_norm):
    t, d = x2.shape
    f = w1.shape[1]
    tm = _tile(seq, 512)
    tf = _tile(f, 1024)
    per_b = seq // tm
    return pl.pallas_call(
        functools.partial(_mlp_kernel, final_norm=final_norm),
        grid=(t // tm, f // tf),
        in_specs=[
            pl.BlockSpec((tm, d), lambda i, j: (i, 0)),
            pl.BlockSpec((1, d), lambda i, j: (0, 0)),
            pl.BlockSpec((1, 1, d), lambda i, j: (i // per_b, 0, 0)),
            pl.BlockSpec((1, 1, d), lambda i, j: (i // per_b, 0, 0)),
            pl.BlockSpec((1, 1, d), lambda i, j: (i // per_b, 0, 0)),
            pl.BlockSpec((d, tf), lambda i, j: (0, j)),
            pl.BlockSpec((tf, d), lambda i, j: (j, 0)),
            pl.BlockSpec((1, d), lambda i, j: (0, 0)),
        ],
        out_specs=pl.BlockSpec((tm, d), lambda i, j: (i, 0)),
        out_shape=jax.ShapeDtypeStruct((t, d), F32),
        scratch_shapes=[pltpu.VMEM((tm, d), BF16), pltpu.VMEM((tm, d), F32)],
        compiler_params=_params("parallel", "arbitrary"),
        name="mlp",
    )(x2, gain.reshape(1, d), shift, scale, gate, w1, w2, fgain.reshape(1, d))


def _silu(x):
    h = 0.5 * x
    return h + h * jnp.tanh(h)


def _hgrn_kernel(q_ref, f_ref, i_ref, g_ref, lbl_ref, og_ref, o_ref, st_ref, *,
                 heads, chunk, n_chunks, layer):
    hd = HEAD_DIM

    @pl.when(pl.program_id(2) == 0)
    def _():
        st_ref[...] = jnp.zeros_like(st_ref)

    logits = lbl_ref[...]
    ex = jnp.exp(logits - jnp.max(logits, axis=0, keepdims=True))
    lb = jnp.sum(ex[:layer + 1], axis=0, keepdims=True) / jnp.sum(ex, axis=0, keepdims=True)
    half_gap = 0.5 * (1.0 - lb)
    f_mid = lb + half_gap
    og = og_ref[...]
    row = lax.broadcasted_iota(jnp.int32, (chunk, heads * hd), 0)
    ti = lax.broadcasted_iota(jnp.int32, (chunk, chunk), 0)
    si = lax.broadcasted_iota(jnp.int32, (chunk, chunk), 1)
    causal = si <= ti

    def body(c, carry):
        r0 = pl.multiple_of(c * chunk, chunk)
        rows = pl.ds(r0, chunk)
        swing = half_gap * jnp.tanh(0.5 * f_ref[rows, :])
        log2_f = jnp.log(f_mid + swing) * LOG2E
        k = half_gap - swing
        cum = log2_f
        d = 1
        while d < chunk:
            cum = cum + jnp.where(row >= d, pltpu.roll(cum, d, 0), 0.0)
            d *= 2
        mid = cum[chunk // 2 - 1:chunk // 2, :]
        tot = cum[chunk - 1:chunk, :]
        qs = _silu(q_ref[rows, :])
        q_in = (qs * jnp.exp2(cum - mid)).astype(BF16)
        k_in = (k * jnp.exp2(mid - cum)).astype(BF16)
        q_dec = (qs * jnp.exp2(cum)).astype(BF16)
        k_st = (k * jnp.exp2(tot - cum)).astype(BF16)
        dec = jnp.exp2(tot)
        v = i_ref[rows, :].astype(BF16)
        gs = _silu(g_ref[rows, :])
        for h in range(heads):
            cs = slice(h * hd, (h + 1) * hd)
            sc = lax.dot_general(q_in[:, cs], k_in[:, cs], (((1,), (1,)), ((), ())),
                                 preferred_element_type=F32)
            sc = jnp.where(causal, sc, 0.0).astype(BF16)
            st = st_ref[h]
            o = jnp.dot(sc, v[:, cs], preferred_element_type=F32)
            o = o + lax.dot_general(q_dec[:, cs], st.astype(BF16), (((1,), (1,)), ((), ())),
                                    preferred_element_type=F32)
            upd = lax.dot_general(v[:, cs], k_st[:, cs], (((0,), (0,)), ((), ())),
                                  preferred_element_type=F32)
            st_ref[h] = dec[:, cs] * st + upd
            inv = lax.rsqrt(jnp.mean(o * o, axis=-1, keepdims=True) + NORM_EPS)
            o_ref[rows, cs] = (o * inv * og[:, cs] * gs[:, cs]).astype(o_ref.dtype)
        return carry

    lax.fori_loop(0, n_chunks, body, 0, unroll=HGRN_UNROLL)


def _hgrn(proj, lb_logits, layer, out_gain, batch, seq):
    t, d4 = proj.shape
    d = d4 // 4
    slots = lb_logits.shape[0]
    hb = _tile(d // HEAD_DIM, HGRN_HEADS_PER_STEP)
    w = hb * HEAD_DIM
    groups = d // w
    tc = _tile(seq, 512)
    nt = seq // tc

    def part(p):
        return pl.BlockSpec((tc, w), lambda b, hg, s: (b * nt + s, p * groups + hg))

    vec = pl.BlockSpec((1, w), lambda b, hg, s: (0, hg))
    slot_vec = pl.BlockSpec((slots, w), lambda b, hg, s: (0, hg))
    return pl.pallas_call(
        functools.partial(_hgrn_kernel, heads=hb, chunk=HGRN_CHUNK, n_chunks=tc // HGRN_CHUNK,
                          layer=layer),
        grid=(batch, groups, nt),
        in_specs=[part(0), part(1), part(2), part(3), slot_vec, vec],
        out_specs=pl.BlockSpec((tc, w), lambda b, hg, s: (b * nt + s, hg)),
        out_shape=jax.ShapeDtypeStruct((t, d), BF16),
        scratch_shapes=[pltpu.VMEM((hb, HEAD_DIM, HEAD_DIM), F32)],
        compiler_params=_params("parallel", "parallel", "arbitrary"),
        name="hgrn",
    )(proj, proj, proj, proj, lb_logits.astype(F32), out_gain.reshape(1, d))


def _softplus2(z):
    m = jnp.maximum(z, 0.0)
    return m + jnp.log(1.0 + jnp.exp2(z - m - m)) * LOG2E


def _attn_kernel(q_ref, k_ref, v_ref, sel_ref, o_ref, vt_ref, qt_ref, z0, z1, zk0, zk1, hl0, hl1, w0, w1,
                 acc_ref, *, heads, tq, half, tiles):
    z_ref, zk_ref, hl_ref, w_ref = (z0, z1), (zk0, zk1), (hl0, hl1), (w0, w1)
    hd = HEAD_DIM
    n_half = tq // half
    assert n_half == 2, "the two-slot pipeline pairs the two masked blocks of a query tile"
    step = pl.program_id(3)

    @pl.when((pl.program_id(2) == 0) & (step == 0))
    def _():
        def transpose_block(c, carry):
            rs = pl.ds(pl.multiple_of(c * hd, hd), hd)
            vt_ref[c] = v_ref[rs, :].astype(F32).T.astype(BF16)
            return carry
        lax.fori_loop(0, vt_ref.shape[0], transpose_block, 0, unroll=8)

    t_in = lax.broadcasted_iota(jnp.int32, (half, tq), 1)
    s_in = lax.broadcasted_iota(jnp.int32, (half, tq), 0)
    tile_ids = [tiles * step + tl for tl in range(tiles)]
    last_kb = [n_half * ti + (n_half - 1) for ti in tile_ids]
    units = [(tl, g) for tl in range(tiles) for g in range(heads)]
    tile_units = [range(tl * heads, (tl + 1) * heads) for tl in range(tiles)]

    def tile_wide(ref, tl):
        return jnp.concatenate([ref[u] for u in tile_units[tl]], axis=1)

    for g, (tl, h) in enumerate(units):
        qt_ref[g] = q_ref[tl * tq:(tl + 1) * tq, h * hd:(h + 1) * hd].astype(F32).T.astype(BF16)

    def pair_start(tl, n):
        return jnp.maximum(last_kb[tl] - n - 1, 0)

    def scores_pair(n):
        for tl in range(tiles):
            ks = pl.ds(pl.multiple_of(pair_start(tl, n) * half, half), 2 * half)
            zz = jnp.dot(k_ref[ks, :], tile_wide(qt_ref, tl), preferred_element_type=F32)
            for g, u in enumerate(tile_units[tl]):
                z_ref[1][u] = zz[:half, g * tq:(g + 1) * tq]
                z_ref[0][u] = zz[half:, g * tq:(g + 1) * tq]

    def split(slot, g, masked_n):
        z = z_ref[slot][g]
        sp = _softplus2(z)
        if masked_n is not None:
            sp = jnp.where((s_in + (n_half - 1 - masked_n) * half) < t_in, sp, 0.0)
        hi, lo = _split_bf16(sp)
        hl_ref[slot][g, :half, :] = hi
        hl_ref[slot][g, half:, :] = lo
        zk_ref[slot][g] = z

    def suffix(slot, g):
        return jnp.dot(sel_ref[...], hl_ref[slot][g], preferred_element_type=F32)

    def weights(slot, g, cs, run, masked_n):
        w = jnp.exp2(zk_ref[slot][g] + cs + run)
        if masked_n is not None:
            w = jnp.where((s_in + (n_half - 1 - masked_n) * half) < t_in, w, 0.0)
        w_ref[slot][g] = w.astype(BF16)
        return run + cs[0:1, :]

    def values_pair(n):
        for tl in range(tiles):
            first = pair_start(tl, n)
            vt = jnp.concatenate([vt_ref[first], vt_ref[first + 1]], axis=1)
            w = jnp.concatenate([tile_wide(w_ref[1], tl), tile_wide(w_ref[0], tl)], axis=0)
            out = jnp.dot(vt, w, preferred_element_type=F32)
            for g, u in enumerate(tile_units[tl]):
                acc_ref[u] += out[:, g * tq:(g + 1) * tq]

    hs = range(len(units))

    def even_iteration(i, runs, masked_split, masked_weights, with_values=True):
        for g in hs:
            split(1, g, masked_split)
        cs = [suffix(0, g) for g in hs]
        if with_values:
            values_pair(i - 2)
        scores_pair(i + 2)
        return [weights(0, g, cs[g], runs[g], masked_weights) for g in hs]

    def odd_iteration(runs, masked_weights):
        for g in hs:
            split(0, g, None)
        cs = [suffix(1, g) for g in hs]
        return [weights(1, g, cs[g], runs[g], masked_weights) for g in hs]

    acc_ref[...] = jnp.zeros_like(acc_ref)
    scores_pair(0)
    for g in hs:
        split(0, g, 0)
    runs = [jnp.zeros((1, tq), F32) for _ in hs]
    runs = even_iteration(0, runs, 1, 0, with_values=False)
    runs = odd_iteration(runs, 1)

    def keep_walking(carry):
        j, live, _ = carry
        return jnp.logical_and(j < tile_ids[-1], live > 0)

    def body(carry):
        j, _, runs = carry
        runs = [jnp.where(j >= tile_ids[tl], ATTN_EXHAUSTED_LOG2, runs[g]) for g, (tl, _) in enumerate(units)]
        runs = even_iteration(2 + 2 * j, runs, None, None)
        runs = odd_iteration(runs, None)
        slowest = functools.reduce(jnp.maximum, runs)
        live = (jnp.max(slowest) > ATTN_DEAD_LOG2).astype(jnp.int32)
        return j + 1, live, tuple(runs)

    pairs_done, _, _ = lax.while_loop(keep_walking, body, (jnp.int32(0), jnp.int32(1), tuple(runs)))
    values_pair(n_half * pairs_done)
    for g, (tl, h) in enumerate(units):
        o_ref[tl * tq:(tl + 1) * tq, h * hd:(h + 1) * hd] = acc_ref[g].T.astype(o_ref.dtype)


def _attention(q, kv, batch, seq):
    t, d = q.shape
    hd = HEAD_DIM
    kvh = kv.shape[1] // (2 * hd)
    group = d // hd // kvh
    heads = _tile(group, ATTN_HEADS)
    tq = _tile(seq, ATTN_Q)
    half = _tile(tq, ATTN_HALF)
    tiles = _tile(seq // tq, ATTN_TILES)
    steps = seq // (tq * tiles)
    gsteps = group // heads
    n_units = tiles * heads
    si = lax.broadcasted_iota(jnp.int32, (half, half), 0)
    ji = lax.broadcasted_iota(jnp.int32, (half, half), 1)
    neg_suffix = -(ji >= si).astype(BF16)
    sel = jnp.concatenate([neg_suffix, neg_suffix], axis=1)
    return pl.pallas_call(
        functools.partial(_attn_kernel, heads=heads, tq=tq, half=half, tiles=tiles),
        grid=(batch, kvh, gsteps, steps),
        in_specs=[
            pl.BlockSpec((tiles * tq, heads * hd), lambda b, h, g, i: (b * steps + i, h * gsteps + g)),
            pl.BlockSpec((seq, hd), lambda b, h, g, i: (b, h)),
            pl.BlockSpec((seq, hd), lambda b, h, g, i: (b, kvh + h)),
            pl.BlockSpec((half, 2 * half), lambda b, h, g, i: (0, 0)),
        ],
        out_specs=pl.BlockSpec((tiles * tq, heads * hd), lambda b, h, g, i: (b * steps + i, h * gsteps + g)),
        out_shape=jax.ShapeDtypeStruct((t, d), BF16),
        scratch_shapes=[
            pltpu.VMEM((seq // hd, hd, hd), BF16),
            pltpu.VMEM((n_units, hd, tq), BF16),
            pltpu.VMEM((n_units, half, tq), F32),
            pltpu.VMEM((n_units, half, tq), F32),
            pltpu.VMEM((n_units, half, tq), F32),
            pltpu.VMEM((n_units, half, tq), F32),
            pltpu.VMEM((n_units, 2 * half, tq), BF16),
            pltpu.VMEM((n_units, 2 * half, tq), BF16),
            pltpu.VMEM((n_units, half, tq), BF16),
            pltpu.VMEM((n_units, half, tq), BF16),
            pltpu.VMEM((n_units, hd, tq), F32),
        ],
        compiler_params=_params("parallel", "parallel", "arbitrary", "arbitrary"),
        name="stick_breaking",
    )(q, kv, kv, sel)


def kernel(x, c, ada_w, ada_b, norm_mix, norm_mlp, a_w_in, a_lb_logits, a_out_gain, a_w_out,
           kv_ada_w, kv_ada_b, kv_norm, w_kv, b_w_q, b_w_out, mlp_w1, mlp_w2, final_norm):
    b, s, d = x.shape
    depth = ada_w.shape[0]
    n_a = a_w_in.shape[0]
    x2 = x.reshape(b * s, d)

    mod = _conditioning(c, ada_w, ada_b).reshape(depth, b, 6, 1, d)
    kv_mod = _conditioning(c, kv_ada_w[None], kv_ada_b[None]).reshape(b, 2, 1, d)

    kv = None
    for layer in range(depth):
        sh1, sc1, g1, sh2, sc2, g2 = (mod[layer, :, p] for p in range(6))
        if layer < n_a:
            proj = _norm_proj(x2, norm_mix[layer], sh1, sc1, _weight_bf16(a_w_in, layer), s, F32)
            o = _hgrn(proj, a_lb_logits, layer, a_out_gain[layer], b, s)
            w_out = _weight_bf16(a_w_out, layer)
        else:
            j = layer - n_a
            q = _norm_proj(x2, norm_mix[layer], sh1, sc1, _weight_bf16(b_w_q, j), s, BF16,
                           out_scale=LOG2E * HEAD_DIM ** -0.5)
            o = _attention(q, kv, b, s)
            w_out = _weight_bf16(b_w_out, j)
        x2 = _proj_residual(o, w_out, x2, g1, s)
        last = layer == depth - 1
        x2 = _mlp(x2, norm_mlp[layer], sh2, sc2, g2, _weight_bf16(mlp_w1, layer),
                  _weight_bf16(mlp_w2, layer), final_norm, s, final_norm=last)
        if layer == n_a - 1:
            kv = _norm_proj(x2, kv_norm, kv_mod[:, 0], kv_mod[:, 1], _weight_bf16(w_kv), s, BF16)
    return x2.reshape(b, s, d)
```
